```python
import jax
import jax.numpy as jnp
from jax import lax
import numpy as np

D_MODEL = 2048
BATCH = 4
SEQ = 4096
DEPTH = 1

CTX_LEN = 256
GRID_W = 64
EPS = 1e-6

D_CONV = 2048
CONV_W = 3

N_QK_HEADS = 16
N_V_HEADS = 32
HEAD_K = 128
HEAD_V = 128
QKV_CONV_W = 3
CHUNK = 64

QK_DIM = N_QK_HEADS * HEAD_K
V_DIM = N_V_HEADS * HEAD_V
QKV_DIM = 2 * QK_DIM + V_DIM
STATE_COLS = QKV_DIM + 4 * N_V_HEADS
Z_OFF = STATE_COLS
CONV_OFF = Z_OFF + V_DIM
GATE_OFF = CONV_OFF + 3 * D_CONV
IN_COLS = GATE_OFF + 2 * D_MODEL

N_EXPERTS = 256
N_GROUPS = 8
TOPK_GROUPS = 4
TOP_K = 8
D_EXPERT = 512
D_SHARED = 512
ROUTED_SCALE = 2.5
BLOCK_ROWS = 128

kernel_name = "hybrid_shortconv_gdn_moe_dit_block"


def rmsnorm(x, w):
    xf = x.astype(jnp.float32)
    y = xf * lax.rsqrt(jnp.mean(xf * xf, axis=-1, keepdims=True) + EPS)
    return (y * w.astype(jnp.float32)).astype(x.dtype)


def modulate(h, shift, scale):
    return h * (1 + scale) + shift


def l2norm(t):
    return t * lax.rsqrt(jnp.sum(t * t, axis=-1, keepdims=True) + EPS)


def centred_dwconv(u, w, axis):
    width = w.shape[0]
    r = width // 2
    pad = [(0, 0)] * u.ndim
    pad[axis] = (r, r)
    up = jnp.pad(u, pad)
    n = u.shape[axis]
    out = lax.slice_in_dim(up, 0, n, axis=axis) * w[0]
    for i in range(1, width):
        out = out + lax.slice_in_dim(up, i, i + n, axis=axis) * w[i]
    return out


def gdn_prepare(p, conv_w, a_log, dt_bias):
    b, n = p.shape[:2]
    f32 = jnp.float32
    qkv = jax.nn.silu(centred_dwconv(p[..., :QKV_DIM], conv_w, axis=1)).astype(f32)
    rep = N_V_HEADS // N_QK_HEADS
    q = jnp.repeat(l2norm(qkv[..., :QK_DIM].reshape(b, n, N_QK_HEADS, HEAD_K)), rep, axis=2)
    k = jnp.repeat(l2norm(qkv[..., QK_DIM:2 * QK_DIM].reshape(b, n, N_QK_HEADS, HEAD_K)), rep, axis=2)
    v = qkv[..., 2 * QK_DIM:].reshape(b, n, N_V_HEADS, HEAD_V)
    ab = p[..., QKV_DIM:STATE_COLS].astype(f32).reshape(b, n, 4, N_V_HEADS)
    a_rate = jnp.exp(a_log.astype(f32))
    dtb = dt_bias.astype(f32)
    g_f = -a_rate[0] * jax.nn.softplus(ab[:, :, 0] + dtb[0])
    beta_f = jax.nn.sigmoid(ab[:, :, 1])
    g_b = -a_rate[1] * jax.nn.softplus(ab[:, :, 2] + dtb[1])
    beta_b = jax.nn.sigmoid(ab[:, :, 3])
    return q, k, v, g_f, beta_f, g_b, beta_b


def gdn_chunked(q, k, v, g, beta, s0):
    b, n_pos, h, dk = q.shape
    dv = v.shape[-1]
    n = n_pos // CHUNK
    f32 = jnp.float32

    def chunks(t):
        t = t.astype(f32).reshape((b, n, CHUNK, h) + t.shape[3:])
        return jnp.moveaxis(t, 3, 1)

    q = chunks(q) * (dk ** -0.5)
    k = chunks(k)
    v = chunks(v)
    beta = chunks(beta)
    g = jnp.cumsum(chunks(g), axis=-1)
    kb = k * beta[..., None]
    vb = v * beta[..., None]
    lower = jnp.tril(jnp.ones((CHUNK, CHUNK), bool))
    strict = jnp.tril(jnp.ones((CHUNK, CHUNK), bool), -1)
    diff = g[..., :, None] - g[..., None, :]
    decay = jnp.where(lower, jnp.exp(jnp.where(lower, diff, 0.0)), 0.0)
    a_mat = jnp.where(strict, jnp.einsum('bhnid,bhnjd->bhnij', kb, k) * decay, 0.0)
    m = a_mat + jnp.eye(CHUNK, dtype=f32)
    eg = jnp.exp(g)
    rhs = jnp.concatenate([vb, kb * eg[..., None]], axis=-1)
    sol = lax.linalg.triangular_solve(m, rhs, left_side=True, lower=True, unit_diagonal=True)
    u, w = sol[..., :dv], sol[..., dv:]
    qk = jnp.einsum('bhnid,bhnjd->bhnij', q, k) * decay
    q_dec = q * eg[..., None]
    g_last = g[..., -1]
    k_dec = k * jnp.exp(g_last[..., None] - g)[..., None]
    xs = tuple(jnp.moveaxis(t, 2, 0) for t in (u, w, qk, q_dec, k_dec, jnp.exp(g_last)))

    def step(s, inp):
        u_n, w_n, qk_n, qd_n, kd_n, gl_n = inp
        v_new = u_n - jnp.einsum('bhck,bhkv->bhcv', w_n, s)
        o_n = jnp.einsum('bhck,bhkv->bhcv', qd_n, s) + jnp.einsum('bhij,bhjv->bhiv', qk_n, v_new)
        s = s * gl_n[..., None, None] + jnp.einsum('bhck,bhcv->bhkv', kd_n, v_new)
        return s, o_n

    s_final, o = lax.scan(step, s0.astype(f32), xs)
    o = jnp.moveaxis(jnp.moveaxis(o, 0, 2), 1, 3).reshape(b, n_pos, h, dv)
    return o, s_final


def gdn_bidir(q, k, v, g_f, beta_f, g_b, beta_b, s0_f, s0_b):
    flip = lambda t: jnp.flip(t, axis=1)
    o_f, s_f = gdn_chunked(q, k, v, g_f, beta_f, s0_f)
    o_b, s_b = gdn_chunked(flip(q), flip(k), flip(v), flip(g_b), flip(beta_b), s0_b)
    return o_f + flip(o_b), s_f, s_b


def gated_rmsnorm(o, z, w):
    zf = z.astype(jnp.float32).reshape(o.shape)
    y = o * lax.rsqrt(jnp.mean(o * o, axis=-1, keepdims=True) + EPS) * w.astype(jnp.float32) * jax.nn.silu(zf)
    return y.reshape(o.shape[:2] + (V_DIM,)).astype(z.dtype)


def mixer_merge(p, o, conv_w, gnorm_w, w_pa, w_pb, w_o, on_grid):
    y_b = gated_rmsnorm(o, p[..., Z_OFF:CONV_OFF], gnorm_w) @ w_pb
    gate_b, gate_c, x_in = jnp.split(p[..., CONV_OFF:GATE_OFF], 3, axis=-1)
    u = gate_c * x_in
    if on_grid:
        b, n, ch = u.shape
        rows = n // GRID_W
        conv_u = centred_dwconv(u.reshape(b, rows, GRID_W, ch), conv_w, axis=2).reshape(b, n, ch)
    else:
        conv_u = centred_dwconv(u, conv_w, axis=1)
    y_a = (gate_b * conv_u) @ w_pa
    mix = (jax.nn.sigmoid(p[..., GATE_OFF:GATE_OFF + D_MODEL]) * y_a
           + jax.nn.sigmoid(p[..., GATE_OFF + D_MODEL:]) * y_b)
    return mix @ w_o


def moe_ffn(h, l, w_router, router_bias, w_gate, w_up, w_down, ws_gate, ws_up, ws_down):
    t = h.shape[0]
    scores = jax.nn.sigmoid(jnp.dot(h.astype(jnp.float32), w_router[l].astype(jnp.float32)))
    biased = scores + router_bias[l].astype(jnp.float32)
    per_group = N_EXPERTS // N_GROUPS
    group_score = lax.top_k(biased.reshape(t, N_GROUPS, per_group), 2)[0].sum(-1)
    top_groups = lax.top_k(group_score, TOPK_GROUPS)[1]
    group_mask = jnp.any(top_groups[..., None] == jnp.arange(N_GROUPS), axis=1)
    expert_mask = jnp.repeat(group_mask, per_group, axis=1)
    top_idx = lax.top_k(jnp.where(expert_mask, biased, -jnp.inf), TOP_K)[1]
    top_w = jnp.take_along_axis(scores, top_idx, axis=1)
    top_w = top_w / jnp.sum(top_w, axis=-1, keepdims=True) * ROUTED_SCALE

    n_assign = t * TOP_K
    n_blocks = (n_assign + BLOCK_ROWS - 1) // BLOCK_ROWS + N_EXPERTS
    n_rows = n_blocks * BLOCK_ROWS
    e_flat = top_idx.reshape(n_assign).astype(jnp.int32)
    tok_flat = jnp.repeat(jnp.arange(t, dtype=jnp.int32), TOP_K)
    order = jnp.argsort(e_flat)
    e_sorted = e_flat[order]
    counts = jnp.zeros((N_EXPERTS,), jnp.int32).at[e_flat].add(1)
    padded = (counts + BLOCK_ROWS - 1) // BLOCK_ROWS * BLOCK_ROWS
    start = jnp.cumsum(counts) - counts
    padded_end = jnp.cumsum(padded)
    padded_start = padded_end - padded
    dest = padded_start[e_sorted] + jnp.arange(n_assign, dtype=jnp.int32) - start[e_sorted]
    tok_buf = jnp.zeros((n_rows,), jnp.int32).at[dest].set(tok_flat[order])
    w_buf = jnp.zeros((n_rows,), h.dtype).at[dest].set(top_w.reshape(n_assign)[order].astype(h.dtype))
    block_expert = jnp.minimum(
        jnp.searchsorted(padded_end, jnp.arange(n_blocks, dtype=jnp.int32) * BLOCK_ROWS, side='right'),
        N_EXPERTS - 1).astype(jnp.int32)

    def expert_block(acc, blk):
        toks, wts, e = blk
        xb = h[toks]
        hid = jax.nn.silu(xb @ w_gate[l, e]) * (xb @ w_up[l, e])
        return acc.at[toks].add((hid @ w_down[l, e]) * wts[:, None]), None

    routed, _ = lax.scan(expert_block, jnp.zeros_like(h),
                         (tok_buf.reshape(n_blocks, BLOCK_ROWS), w_buf.reshape(n_blocks, BLOCK_ROWS), block_expert))
    shared = (jax.nn.silu(h @ ws_gate[l]) * (h @ ws_up[l])) @ ws_down[l]
    return routed + shared


def setup_inputs(seed: int = 0) -> dict:
    key = jax.random.key(seed)
    ks = jax.random.split(key, 26)
    f32 = jnp.float32
    L = DEPTH
    D = D_MODEL

    def nrm(k, shape, scale):
        return jax.random.normal(k, shape, f32) * scale

    a_log = jnp.log(jax.random.uniform(ks[11], (L, 2, N_V_HEADS), f32, 1.0, 16.0))
    dt = jnp.exp(jax.random.uniform(ks[12], (L, 2, N_V_HEADS), f32,
                                    float(np.log(1e-3)), float(np.log(1e-1))))
    dt_bias = dt + jnp.log(-jnp.expm1(-dt))
    return {
        "x": nrm(ks[0], (BATCH, SEQ, D), 1.0),
        "c": nrm(ks[1], (BATCH, D), 1.0),
        "ctx": nrm(ks[2], (BATCH, CTX_LEN, D), 1.0),
        "c_ctx": nrm(ks[3], (D,), 1.0),
        "w_ada": nrm(ks[4], (L, D, 6 * D), 0.3 * D ** -0.5),
        "b_ada": nrm(ks[5], (L, 6 * D), 0.02),
        "norm_mix": 1.0 + nrm(ks[6], (L, D), 0.02),
        "norm_ffn": 1.0 + nrm(ks[7], (L, D), 0.02),
        "w_in": nrm(ks[8], (L, D, IN_COLS), D ** -0.5),
        "conv_a": nrm(ks[9], (L, CONV_W, D_CONV), CONV_W ** -0.5),
        "conv_qkv": nrm(ks[10], (L, QKV_CONV_W, QKV_DIM), QKV_CONV_W ** -0.5),
        "a_log": a_log,
        "dt_bias": dt_bias,
        "gdn_norm": 1.0 + nrm(ks[13], (L, HEAD_V), 0.02),
        "w_proj_a": nrm(ks[14], (L, D_CONV, D), D_CONV ** -0.5),
        "w_proj_b": nrm(ks[15], (L, V_DIM, D), V_DIM ** -0.5),
        "w_out": nrm(ks[16], (L, D, D), D ** -0.5),
        "w_router": nrm(ks[17], (L, D, N_EXPERTS), D ** -0.5),
        "router_bias": nrm(ks[18], (L, N_EXPERTS), 0.01),
        "w_gate": nrm(ks[19], (L, N_EXPERTS, D, D_EXPERT), D ** -0.5),
        "w_up": nrm(ks[20], (L, N_EXPERTS, D, D_EXPERT), D ** -0.5),
        "w_down": nrm(ks[21], (L, N_EXPERTS, D_EXPERT, D), D_EXPERT ** -0.5),
        "ws_gate": nrm(ks[22], (L, D, D_SHARED), D ** -0.5),
        "ws_up": nrm(ks[23], (L, D, D_SHARED), D ** -0.5),
        "ws_down": nrm(ks[24], (L, D_SHARED, D), D_SHARED ** -0.5),
        "final_norm": 1.0 + nrm(ks[25], (D,), 0.02),
    }


def reference(x, c, ctx, c_ctx, w_ada, b_ada, norm_mix, norm_ffn, w_in, conv_a, conv_qkv, a_log,
              dt_bias, gdn_norm, w_proj_a, w_proj_b, w_out, w_router, router_bias, w_gate, w_up,
              w_down, ws_gate, ws_up, ws_down, final_norm):
    b, seq, d = x.shape
    xc = ctx
    for l in range(DEPTH):
        last = l == DEPTH - 1
        mod = jnp.split(jax.nn.silu(c) @ w_ada[l] + b_ada[l], 6, axis=-1)
        mod_c = jnp.split(jax.nn.silu(c_ctx) @ w_ada[l] + b_ada[l], 6, axis=-1)

        h = modulate(rmsnorm(x, norm_mix[l]), mod[0][:, None], mod[1][:, None])
        hc = modulate(rmsnorm(xc, norm_mix[l]), mod_c[0], mod_c[1])
        p = h @ w_in[l]
        pc = hc @ (w_in[l][:, :STATE_COLS] if last else w_in[l])

        q, k, v, g_f, beta_f, g_b, beta_b = gdn_prepare(p, conv_qkv[l], a_log[l], dt_bias[l])
        qc, kc, vc, gc_f, betac_f, gc_b, betac_b = gdn_prepare(pc, conv_qkv[l], a_log[l], dt_bias[l])
        s0 = jnp.zeros((b, N_V_HEADS, HEAD_K, HEAD_V), jnp.float32)
        o_ctx, s_ctx_f, s_ctx_b = gdn_bidir(qc, kc, vc, gc_f, betac_f, gc_b, betac_b, s0, s0)
        o_lat, _, _ = gdn_bidir(q, k, v, g_f, beta_f, g_b, beta_b, s_ctx_f, s_ctx_b)

        x = x + mod[2][:, None] * mixer_merge(p, o_lat, conv_a[l], gdn_norm[l], w_proj_a[l],
                                              w_proj_b[l], w_out[l], True)
        if not last:
            xc = xc + mod_c[2] * mixer_merge(pc, o_ctx, conv_a[l], gdn_norm[l], w_proj_a[l],
                                             w_proj_b[l], w_out[l], False)

        h2 = modulate(rmsnorm(x, norm_ffn[l]), mod[3][:, None], mod[4][:, None])
        y = moe_ffn(h2.reshape(b * seq, d), l, w_router, router_bias, w_gate, w_up, w_down,
                    ws_gate, ws_up, ws_down).reshape(b, seq, d)
        x = x + mod[5][:, None] * y
        if not last:
            h2c = modulate(rmsnorm(xc, norm_ffn[l]), mod_c[3], mod_c[4])
            yc = moe_ffn(h2c.reshape(b * CTX_LEN, d), l, w_router, router_bias, w_gate, w_up, w_down,
                         ws_gate, ws_up, ws_down).reshape(b, CTX_LEN, d)
            xc = xc + mod_c[5] * yc
    return rmsnorm(x, final_norm)
```

```python
import functools

import jax
import jax.numpy as jnp
from jax import lax
from jax.experimental import pallas as pl
from jax.experimental.pallas import tpu as pltpu

F32 = jnp.float32
BF16 = jnp.bfloat16
I32 = jnp.int32

EPS = 1e-6
GRID_W = 64
N_QK_HEADS = 16
N_V_HEADS = 32
HEAD_K = 128
HEAD_V = 128
CHUNK = 64
N_EXPERTS = 256
N_GROUPS = 8
TOPK_GROUPS = 4
TOP_K = 8
ROUTED_SCALE = 2.5

QK_DIM = N_QK_HEADS * HEAD_K
V_DIM = N_V_HEADS * HEAD_V
QKV_DIM = 2 * QK_DIM + V_DIM
N_GATE_COLS = 4 * N_V_HEADS

V7X_LANES = 128
V7X_VMEM_LIMIT_BYTES = 56 * 1024 * 1024
EXPERT_BLOCK_ROWS = 256


def _pick(n, target, mult):
    if n <= target:
        return n
    t = (target // mult) * mult
    while t >= mult:
        if n % t == 0:
            return t
        t -= mult
    raise ValueError(f"no tile for {n} (target {target}, multiple {mult})")


def _params(sem, vmem=V7X_VMEM_LIMIT_BYTES):
    return pltpu.CompilerParams(dimension_semantics=sem, vmem_limit_bytes=vmem)


def _sigmoid(x):
    return 1.0 / (1.0 + jnp.exp(-x))


def _silu(x):
    return x * _sigmoid(x)


def _dot(a, b):
    return jnp.dot(a.astype(BF16), b.astype(BF16), preferred_element_type=F32)


def _dot_nt(a, b):
    return lax.dot_general(a.astype(BF16), b.astype(BF16), (((1,), (1,)), ((), ())),
                           preferred_element_type=F32)


def _dot_tn(a, b):
    return lax.dot_general(a.astype(BF16), b.astype(BF16), (((0,), (0,)), ((), ())),
                           preferred_element_type=F32)


def _split3(x):
    hi = x.astype(BF16)
    r = x - hi.astype(F32)
    mid = r.astype(BF16)
    lo = (r - mid.astype(F32)).astype(BF16)
    return hi, mid, lo


def _ada_kernel(c_ref, w_ref, b_ref, o_ref):
    a = _silu(c_ref[...])
    o_ref[...] = _dot(a, w_ref[...]) + b_ref[...]


def _ada(cpad, w, b):
    rows, d = cpad.shape
    n = w.shape[1]
    tn = _pick(n, 1024, V7X_LANES)
    return pl.pallas_call(
        _ada_kernel,
        out_shape=jax.ShapeDtypeStruct((rows, n), F32),
        grid=(n // tn,),
        in_specs=[pl.BlockSpec((rows, d), lambda j: (0, 0)),
                  pl.BlockSpec((d, tn), lambda j: (0, j)),
                  pl.BlockSpec((1, tn), lambda j: (0, j))],
        out_specs=pl.BlockSpec((rows, tn), lambda j: (0, j)),
        compiler_params=_params(("parallel",)),
        name="ada_mod",
    )(cpad, w, b)


def _norm_mod_kernel(x_ref, w_ref, mod_ref, o_ref, *, shift_idx, scale_idx):
    x = x_ref[0]
    y = x * lax.rsqrt(jnp.mean(x * x, axis=-1, keepdims=True) + EPS) * w_ref[...]
    m = mod_ref[0]
    o_ref[0] = (y * (1.0 + m[scale_idx:scale_idx + 1]) + m[shift_idx:shift_idx + 1]).astype(o_ref.dtype)


def _norm_mod(x, w, mod3, row_of_batch, shift_idx, scale_idx, out_dtype):
    b, n, d = x.shape
    ts = _pick(n, 512, 16)
    return pl.pallas_call(
        functools.partial(_norm_mod_kernel, shift_idx=shift_idx, scale_idx=scale_idx),
        out_shape=jax.ShapeDtypeStruct((b, n, d), out_dtype),
        grid=(b, n // ts),
        in_specs=[pl.BlockSpec((1, ts, d), lambda i, j: (i, j, 0)),
                  pl.BlockSpec((1, d), lambda i, j: (0, 0)),
                  pl.BlockSpec((1, 6, d), lambda i, j: (row_of_batch(i), 0, 0))],
        out_specs=pl.BlockSpec((1, ts, d), lambda i, j: (i, j, 0)),
        compiler_params=_params(("parallel", "parallel")),
        name="norm_mod",
    )(x, w, mod3)


def _mm_kernel(a_ref, w_ref, o_ref):
    o_ref[...] = jnp.dot(a_ref[...], w_ref[...], preferred_element_type=F32).astype(o_ref.dtype)


def _mm(a, w, out_dtype, name):
    m, k = a.shape
    n = w.shape[1]
    tm = _pick(m, 1024, 16)
    tn = _pick(n, 1024, V7X_LANES)
    return pl.pallas_call(
        _mm_kernel,
        out_shape=jax.ShapeDtypeStruct((m, n), out_dtype),
        grid=(n // tn, m // tm),
        in_specs=[pl.BlockSpec((tm, k), lambda j, i: (i, 0)),
                  pl.BlockSpec((k, tn), lambda j, i: (0, j))],
        out_specs=pl.BlockSpec((tm, tn), lambda j, i: (i, j)),
        compiler_params=_params(("parallel", "parallel")),
        name=name,
    )(a, w)


def _gates_kernel(ab_ref, prm_ref, col_ref, row_ref, *, rows):
    sub = 2 * CHUNK
    n_sub = rows // sub
    prm = prm_ref[...]
    alog = prm[0:1]
    dtb = prm[1:2]
    lane = lax.broadcasted_iota(I32, (sub, N_GATE_COLS), 1)
    is_g = (lane // N_V_HEADS) % 2 == 0
    is_bwd = lane >= 2 * N_V_HEADS
    r = lax.broadcasted_iota(I32, (sub, sub), 0)
    c = lax.broadcasted_iota(I32, (sub, sub), 1)
    same = (r // CHUNK) == (c // CHUNK)
    cum_f = jnp.where(same & (c <= r), 1.0, 0.0).astype(BF16)
    cum_b = jnp.where(same & (c >= r), 1.0, 0.0).astype(BF16)
    for s in range(n_sub):
        x = ab_ref[0, s * sub:(s + 1) * sub, :]
        xs = x + dtb
        softplus = jnp.maximum(xs, 0.0) + jnp.log(1.0 + jnp.exp(-jnp.abs(xs)))
        g = -jnp.exp(alog) * softplus
        beta = _sigmoid(x)
        hi, mid, lo = _split3(g)
        pf = (jnp.dot(cum_f, hi, preferred_element_type=F32) + jnp.dot(cum_f, mid, preferred_element_type=F32)
              + jnp.dot(cum_f, lo, preferred_element_type=F32))
        pb = (jnp.dot(cum_b, hi, preferred_element_type=F32) + jnp.dot(cum_b, mid, preferred_element_type=F32)
              + jnp.dot(cum_b, lo, preferred_element_type=F32))
        out = jnp.where(is_g, jnp.where(is_bwd, pb, pf), beta)
        col_ref[0, s * sub:(s + 1) * sub, :] = out
        row_ref[0, :, s * sub:(s + 1) * sub] = out.T


def _gates(ab, prm):
    b, n, g = ab.shape
    rows = _pick(n, 512, 2 * CHUNK)
    return pl.pallas_call(
        functools.partial(_gates_kernel, rows=rows),
        out_shape=(jax.ShapeDtypeStruct((b, n, g), F32), jax.ShapeDtypeStruct((b, g, n), F32)),
        grid=(b, n // rows),
        in_specs=[pl.BlockSpec((1, rows, g), lambda i, j: (i, j, 0)),
                  pl.BlockSpec((2, g), lambda i, j: (0, 0))],
        out_specs=(pl.BlockSpec((1, rows, g), lambda i, j: (i, j, 0)),
                   pl.BlockSpec((1, g, rows), lambda i, j: (i, 0, j))),
        compiler_params=_params(("parallel", "parallel")),
        name="gate_prep",
    )(ab, prm)


def _conv_silu_tile(src_ref, w, base, rows, n_total):
    x = src_ref[0, pl.ds(base, rows), :].astype(F32)
    pb = pl.multiple_of(jnp.maximum(base - 16, 0), 16)
    nb = pl.multiple_of(jnp.minimum(base + rows, n_total - 16), 16)
    prev = src_ref[0, pl.ds(pb, 16), :].astype(F32)[15:16]
    nxt = src_ref[0, pl.ds(nb, 16), :].astype(F32)[0:1]
    prev = jnp.where(base > 0, prev, 0.0)
    nxt = jnp.where(base + rows < n_total, nxt, 0.0)
    ridx = lax.broadcasted_iota(I32, x.shape, 0)
    xp = jnp.where(ridx == 0, prev, pltpu.roll(x, 1, 0))
    xn = jnp.where(ridx == rows - 1, nxt, pltpu.roll(x, rows - 1, 0))
    y = xp * w[0:1] + x * w[1:2] + xn * w[2:3]
    return _silu(y)


def _l2norm_rows(t):
    return t * lax.rsqrt(jnp.sum(t * t, axis=-1, keepdims=True) + EPS)


def _gdn_kernel(q_ref, k_ref, v_ref, z_ref, gc_ref, gt_ref,
                qc_ref, kc_ref, vc_ref, gcc_ref, gtc_ref,
                cq_ref, ck_ref, cv_ref, gn_ref,
                y_ref,
                qs, ks, vs, osum, st, *, n_lat, n_ctx):
    hq = pl.program_id(1)
    rep = N_V_HEADS // N_QK_HEADS
    wq = cq_ref[...]
    wk = ck_ref[...]
    wv = cv_ref[...]

    def prep(src_q, src_k, src_v, n_total, row0):
        tile = _pick(n_total, 256, 16)

        def body(t, carry):
            base = pl.multiple_of(t * tile, tile)
            dst = pl.multiple_of(row0 + base, 16)
            qs[pl.ds(dst, tile), :] = _l2norm_rows(_conv_silu_tile(src_q, wq, base, tile, n_total)) * (HEAD_K ** -0.5)
            ks[pl.ds(dst, tile), :] = _l2norm_rows(_conv_silu_tile(src_k, wk, base, tile, n_total))
            vs[pl.ds(dst, tile), :] = _conv_silu_tile(src_v, wv, base, tile, n_total)
            return carry

        lax.fori_loop(0, n_total // tile, body, 0)

    prep(qc_ref, kc_ref, vc_ref, n_ctx, 0)
    prep(q_ref, k_ref, v_ref, n_lat, n_ctx)

    st[...] = jnp.zeros(st.shape, F32)
    osum[...] = jnp.zeros(osum.shape, F32)

    ri = lax.broadcasted_iota(I32, (CHUNK, CHUNK), 0)
    ci = lax.broadcasted_iota(I32, (CHUNK, CHUNK), 1)
    eye = jnp.where(ri == ci, 1.0, 0.0)
    incl = (ri >= ci, ri <= ci)
    strict = (ri > ci, ri < ci)
    last = (CHUNK - 1, 0)
    lane = lax.broadcasted_iota(I32, (CHUNK, N_GATE_COLS), 1)

    def chunk_step(c, dirn, gcol_ref, grow_ref, row0, write_out):
        r = pl.multiple_of(row0 + c * CHUNK, CHUNK)
        q = qs[pl.ds(r, CHUNK), :]
        k = ks[pl.ds(r, CHUNK), :]
        vv = vs[pl.ds(r, CHUNK), :]
        kk = _dot_nt(k, k)
        qk = _dot_nt(q, k)
        gtile = gcol_ref[0, pl.ds(pl.multiple_of(c * CHUNK, CHUNK), CHUNK), :]
        for head in range(rep):
            col_g = (2 * dirn) * N_V_HEADS + hq * rep + head
            gcol = jnp.sum(jnp.where(lane == col_g, gtile, 0.0), axis=1, keepdims=True)
            bcol = jnp.sum(jnp.where(lane == col_g + N_V_HEADS, gtile, 0.0), axis=1, keepdims=True)
            grow = grow_ref[0, 2 * dirn, head, pl.ds(c, 1), :]
            diff = gcol - grow
            decay = jnp.where(incl[dirn], jnp.exp(jnp.where(incl[dirn], diff, 0.0)), 0.0)
            a_mat = jnp.where(strict[dirn], bcol * kk * decay, 0.0)
            bm = -a_mat
            s_acc = eye + bm
            bp = _dot(bm, bm)
            for _ in range(4):
                s_acc = s_acc + _dot(bp, s_acc)
                bp = _dot(bp, bp)
            t_inv = s_acc + _dot(bp, s_acc)
            v = vv[:, head * HEAD_V:(head + 1) * HEAD_V]
            eg = jnp.exp(gcol)
            u = _dot(t_inv, v * bcol)
            w = _dot(t_inv, k * (bcol * eg))
            chain = dirn * rep + head
            s_old = st[chain]
            v_new = u - _dot(w, s_old)
            if write_out:
                o = _dot(q * eg, s_old) + _dot(qk * decay, v_new)
                lr = pl.multiple_of(c * CHUNK, CHUNK)
                osum[pl.ds(lr, CHUNK), head * HEAD_V:(head + 1) * HEAD_V] += o
            glast = gcol[last[dirn]:last[dirn] + 1, :]
            k_dec = k * jnp.exp(glast - gcol)
            st[chain] = s_old * jnp.exp(glast) + _dot_tn(k_dec, v_new)

    def scan(n_chunks, gcol_ref, grow_ref, row0, write_out):
        def body(n, carry):
            chunk_step(n, 0, gcol_ref, grow_ref, row0, write_out)
            chunk_step(n_chunks - 1 - n, 1, gcol_ref, grow_ref, row0, write_out)
            return carry

        lax.fori_loop(0, n_chunks, body, 0)

    scan(n_ctx // CHUNK, gcc_ref, gtc_ref, 0, False)
    scan(n_lat // CHUNK, gc_ref, gt_ref, n_ctx, True)

    gn = gn_ref[...]
    tile = _pick(n_lat, 256, 16)

    def epi(t, carry):
        base = pl.multiple_of(t * tile, tile)
        o = osum[pl.ds(base, tile), :]
        z = z_ref[0, pl.ds(base, tile), :].astype(F32)
        for head in range(rep):
            sl = slice(head * HEAD_V, (head + 1) * HEAD_V)
            oh = o[:, sl]
            y = oh * lax.rsqrt(jnp.mean(oh * oh, axis=-1, keepdims=True) + EPS) * gn * _silu(z[:, sl])
            y_ref[0, pl.ds(base, tile), sl] = y.astype(y_ref.dtype)
        return carry

    lax.fori_loop(0, n_lat // tile, epi, 0)


def _gdn(p_qkv, p_z, gcol, grow, pc_qkv, gcol_c, grow_c, conv_qkv, gdn_norm):
    b, n_lat, _ = p_qkv.shape
    n_ctx = pc_qkv.shape[1]
    rep = N_V_HEADS // N_QK_HEADS
    vw = rep * HEAD_V
    kq0 = QK_DIM // HEAD_K
    v0 = 2 * QK_DIM // vw
    nc_lat, nc_ctx = n_lat // CHUNK, n_ctx // CHUNK
    grow5 = grow.reshape(b, 4, N_QK_HEADS, rep, nc_lat, CHUNK)
    grow5_c = grow_c.reshape(b, 4, N_QK_HEADS, rep, nc_ctx, CHUNK)

    def seq_specs(n):
        return [pl.BlockSpec((1, n, HEAD_K), lambda i, h: (i, 0, h)),
                pl.BlockSpec((1, n, HEAD_K), lambda i, h: (i, 0, kq0 + h)),
                pl.BlockSpec((1, n, vw), lambda i, h: (i, 0, v0 + h))]

    def gate_specs(n, nc):
        return [pl.BlockSpec((1, n, N_GATE_COLS), lambda i, h: (i, 0, 0)),
                pl.BlockSpec((1, 4, None, rep, nc, CHUNK), lambda i, h: (i, 0, h, 0, 0, 0))]

    in_specs = (seq_specs(n_lat)
                + [pl.BlockSpec((1, n_lat, vw), lambda i, h: (i, 0, h))]
                + gate_specs(n_lat, nc_lat)
                + seq_specs(n_ctx)
                + gate_specs(n_ctx, nc_ctx)
                + [pl.BlockSpec((3, HEAD_K), lambda i, h: (0, h)),
                   pl.BlockSpec((3, HEAD_K), lambda i, h: (0, kq0 + h)),
                   pl.BlockSpec((3, vw), lambda i, h: (0, v0 + h)),
                   pl.BlockSpec((1, HEAD_V), lambda i, h: (0, 0))])
    n_all = n_ctx + n_lat
    return pl.pallas_call(
        functools.partial(_gdn_kernel, n_lat=n_lat, n_ctx=n_ctx),
        out_shape=jax.ShapeDtypeStruct((b, n_lat, V_DIM), BF16),
        grid=(b, N_QK_HEADS),
        in_specs=in_specs,
        out_specs=pl.BlockSpec((1, n_lat, vw), lambda i, h: (i, 0, h)),
        scratch_shapes=[pltpu.VMEM((n_all, HEAD_K), F32),
                        pltpu.VMEM((n_all, HEAD_K), F32),
                        pltpu.VMEM((n_all, vw), F32),
                        pltpu.VMEM((n_lat, vw), F32),
                        pltpu.VMEM((2 * rep, HEAD_K, HEAD_V), F32)],
        compiler_params=_params(("parallel", "parallel")),
        name="gdn_bidir",
    )(p_qkv, p_qkv, p_qkv, p_z, gcol, grow5,
      pc_qkv, pc_qkv, pc_qkv, gcol_c, grow5_c,
      conv_qkv, conv_qkv, conv_qkv, gdn_norm)


def _conva_kernel(gb_ref, gcx_ref, xin_ref, w_ref, o_ref):
    u = gcx_ref[...].astype(F32) * xin_ref[...].astype(F32)
    rows = u.shape[0]
    pos = lax.broadcasted_iota(I32, u.shape, 0) % GRID_W
    up = jnp.where(pos == 0, 0.0, pltpu.roll(u, 1, 0))
    un = jnp.where(pos == GRID_W - 1, 0.0, pltpu.roll(u, rows - 1, 0))
    w = w_ref[...]
    conv = up * w[0:1] + u * w[1:2] + un * w[2:3]
    o_ref[...] = (gb_ref[...].astype(F32) * conv).astype(o_ref.dtype)


def _conva(p_conv, conv_w):
    t, c3 = p_conv.shape
    dc = c3 // 3
    tr = _pick(t, 512, GRID_W)
    tc = _pick(dc, 512, V7X_LANES)
    nj = dc // tc
    return pl.pallas_call(
        _conva_kernel,
        out_shape=jax.ShapeDtypeStruct((t, dc), BF16),
        grid=(t // tr, nj),
        in_specs=[pl.BlockSpec((tr, tc), lambda i, j: (i, j)),
                  pl.BlockSpec((tr, tc), lambda i, j: (i, nj + j)),
                  pl.BlockSpec((tr, tc), lambda i, j: (i, 2 * nj + j)),
                  pl.BlockSpec((3, tc), lambda i, j: (0, j))],
        out_specs=pl.BlockSpec((tr, tc), lambda i, j: (i, j)),
        compiler_params=_params(("parallel", "parallel")),
        name="mixer_a_conv",
    )(p_conv, p_conv, p_conv, conv_w)


def _mixout_kernel(ya_ref, yb_ref, ga_ref, gb_ref, wo_ref, x_ref, mod_ref, o_ref, *, gate_idx):
    mix = (_sigmoid(ga_ref[0].astype(F32)) * ya_ref[0].astype(F32)
           + _sigmoid(gb_ref[0].astype(F32)) * yb_ref[0].astype(F32))
    out = _dot(mix, wo_ref[...])
    m = mod_ref[0]
    o_ref[0] = x_ref[0] + m[gate_idx:gate_idx + 1] * out


def _mixout(ya, yb, p_gate, w_o, x, mod3):
    b, n, d = x.shape
    tm = _pick(n, 512, 16)
    return pl.pallas_call(
        functools.partial(_mixout_kernel, gate_idx=2),
        out_shape=jax.ShapeDtypeStruct((b, n, d), F32),
        grid=(b, n // tm),
        in_specs=[pl.BlockSpec((1, tm, d), lambda i, j: (i, j, 0)),
                  pl.BlockSpec((1, tm, d), lambda i, j: (i, j, 0)),
                  pl.BlockSpec((1, tm, d), lambda i, j: (i, j, 0)),
                  pl.BlockSpec((1, tm, d), lambda i, j: (i, j, 1)),
                  pl.BlockSpec((d, d), lambda i, j: (0, 0)),
                  pl.BlockSpec((1, tm, d), lambda i, j: (i, j, 0)),
                  pl.BlockSpec((1, 6, d), lambda i, j: (i, 0, 0))],
        out_specs=pl.BlockSpec((1, tm, d), lambda i, j: (i, j, 0)),
        compiler_params=_params(("parallel", "parallel")),
        name="merge_outproj",
    )(ya, yb, p_gate, p_gate, w_o, x, mod3)


def _router_kernel(x_ref, w_ref, mod_ref, wr_ref, bias_ref, h_ref, idx_ref, wgt_ref, *, shift_idx, scale_idx):
    x = x_ref[0]
    y = x * lax.rsqrt(jnp.mean(x * x, axis=-1, keepdims=True) + EPS) * w_ref[...]
    m = mod_ref[0]
    h = y * (1.0 + m[scale_idx:scale_idx + 1]) + m[shift_idx:shift_idx + 1]
    h_ref[0] = h
    tm = h.shape[0]
    h_hi = h.astype(BF16)
    h_lo = (h - h_hi.astype(F32)).astype(BF16)
    wr = wr_ref[...]
    w_hi = wr.astype(BF16)
    w_lo = (wr - w_hi.astype(F32)).astype(BF16)
    logits = _dot_nt(w_hi, h_hi) + _dot_nt(w_hi, h_lo) + _dot_nt(w_lo, h_hi)
    scores = _sigmoid(logits)
    biased = scores + bias_ref[...]
    per_group = N_EXPERTS // N_GROUPS
    neg = -jnp.inf
    i32g = lax.broadcasted_iota(I32, (per_group, tm), 0).astype(F32)
    groups, gscore = [], []
    for g in range(N_GROUPS):
        sg = biased[g * per_group:(g + 1) * per_group, :]
        m1 = jnp.max(sg, axis=0, keepdims=True)
        a1 = jnp.min(jnp.where(sg == m1, i32g, float(per_group)), axis=0, keepdims=True)
        m2 = jnp.max(jnp.where(i32g == a1, neg, sg), axis=0, keepdims=True)
        groups.append(sg)
        gscore.append(m1 + m2)
    parts = []
    for g in range(N_GROUPS):
        ahead = jnp.zeros((1, tm), F32)
        for o in range(N_GROUPS):
            if o == g:
                continue
            beats = (gscore[o] >= gscore[g]) if o < g else (gscore[o] > gscore[g])
            ahead = ahead + jnp.where(beats, 1.0, 0.0)
        parts.append(jnp.where(ahead < float(TOPK_GROUPS), groups[g], neg))
    masked = jnp.concatenate(parts, axis=0)
    ei = lax.broadcasted_iota(I32, masked.shape, 0).astype(F32)
    idxs, wgts = [], []
    for _ in range(TOP_K):
        em = jnp.max(masked, axis=0, keepdims=True)
        ea = jnp.min(jnp.where(masked == em, ei, float(N_EXPERTS)), axis=0, keepdims=True)
        hit = ei == ea
        idxs.append(ea)
        wgts.append(jnp.sum(jnp.where(hit, scores, 0.0), axis=0, keepdims=True))
        masked = jnp.where(hit, neg, masked)
    idx = jnp.concatenate(idxs, axis=0)
    wgt = jnp.concatenate(wgts, axis=0)
    wgt = wgt / jnp.sum(wgt, axis=0, keepdims=True) * ROUTED_SCALE
    idx_ref[0] = idx.astype(I32)
    wgt_ref[0] = wgt


def _router(x1, norm_w, mod3, wr_t, bias_col):
    b, n, d = x1.shape
    tm = _pick(n, 256, V7X_LANES)
    nt = n // tm
    return pl.pallas_call(
        functools.partial(_router_kernel, shift_idx=3, scale_idx=4),
        out_shape=(jax.ShapeDtypeStruct((b, n, d), F32),
                   jax.ShapeDtypeStruct((b, TOP_K, n), I32),
                   jax.ShapeDtypeStruct((b, TOP_K, n), F32)),
        grid=(b, nt),
        in_specs=[pl.BlockSpec((1, tm, d), lambda i, j: (i, j, 0)),
                  pl.BlockSpec((1, d), lambda i, j: (0, 0)),
                  pl.BlockSpec((1, 6, d), lambda i, j: (i, 0, 0)),
                  pl.BlockSpec((N_EXPERTS, d), lambda i, j: (0, 0)),
                  pl.BlockSpec((N_EXPERTS, 1), lambda i, j: (0, 0))],
        out_specs=(pl.BlockSpec((1, tm, d), lambda i, j: (i, j, 0)),
                   pl.BlockSpec((1, TOP_K, tm), lambda i, j: (i, 0, j)),
                   pl.BlockSpec((1, TOP_K, tm), lambda i, j: (i, 0, j))),
        compiler_params=_params(("parallel", "parallel")),
        name="ffn_norm_router",
    )(x1, norm_w, mod3, wr_t, bias_col)


def _experts_kernel(be_ref, nv_ref, nu_ref, tok_ref, slot_ref, h_hbm, wg_ref, wu_ref, wd_ref, y_hbm,
                    xbuf, ybuf, gsem, ssem):
    j = pl.program_id(0)
    n_used = nu_ref[0]

    def gather_row(blk, r):
        return pltpu.make_async_copy(h_hbm.at[pl.ds(tok_ref[0, 0, r], 1)],
                                     xbuf.at[blk % 2, pl.ds(r, 1)], gsem.at[blk % 2])

    def scatter_row(blk, r):
        return pltpu.make_async_copy(ybuf.at[blk % 2, pl.ds(r, 1)],
                                     y_hbm.at[pl.ds(slot_ref[0, 0, r], 1)], ssem.at[blk % 2])

    def wait_gather(blk):
        def body(r, c):
            pltpu.make_async_copy(h_hbm.at[pl.ds(0, 1)], xbuf.at[blk % 2, pl.ds(0, 1)], gsem.at[blk % 2]).wait()
            return c
        lax.fori_loop(0, nv_ref[blk], body, 0)

    def wait_scatter(blk):
        def body(r, c):
            pltpu.make_async_copy(ybuf.at[blk % 2, pl.ds(0, 1)], y_hbm.at[pl.ds(0, 1)], ssem.at[blk % 2]).wait()
            return c
        lax.fori_loop(0, nv_ref[blk], body, 0)

    @pl.when(j == 0)
    def _():
        xbuf[...] = jnp.zeros(xbuf.shape, F32)

    @pl.when(j < n_used)
    def _():
        def body(r, c):
            gather_row(j, r).start()
            return c
        lax.fori_loop(0, nv_ref[j], body, 0)

    @pl.when((j >= 1) & (j - 1 < n_used))
    def _():
        i = j - 1
        wait_gather(i)

        @pl.when(i >= 2)
        def _():
            wait_scatter(i - 2)

        x = xbuf[i % 2].astype(BF16)
        hid = _silu(jnp.dot(x, wg_ref[0].astype(BF16), preferred_element_type=F32)) * jnp.dot(
            x, wu_ref[0].astype(BF16), preferred_element_type=F32)
        ybuf[i % 2] = jnp.dot(hid.astype(BF16), wd_ref[0].astype(BF16), preferred_element_type=F32)

        def body(r, c):
            scatter_row(i, r).start()
            return c
        lax.fori_loop(0, nv_ref[i], body, 0)

        @pl.when(i == n_used - 1)
        def _():
            wait_scatter(i)

            @pl.when(i >= 1)
            def _():
                wait_scatter(i - 1)


def _experts(h2, block_expert, n_valid, n_used, tok_buf, slot_buf, w_gate, w_up, w_down):
    t, d = h2.shape
    de = w_gate.shape[2]
    n_blocks, _, bm = tok_buf.shape

    def cur(j):
        return jnp.maximum(j - 1, 0)

    grid_spec = pltpu.PrefetchScalarGridSpec(
        num_scalar_prefetch=3,
        grid=(n_blocks + 1,),
        in_specs=[
            pl.BlockSpec((1, 1, bm), lambda j, be, nv, nu: (jnp.minimum(j, n_blocks - 1), 0, 0),
                         memory_space=pltpu.SMEM),
            pl.BlockSpec((1, 1, bm), lambda j, be, nv, nu: (cur(j), 0, 0), memory_space=pltpu.SMEM),
            pl.BlockSpec(memory_space=pl.ANY),
            pl.BlockSpec((1, d, de), lambda j, be, nv, nu: (be[cur(j)], 0, 0)),
            pl.BlockSpec((1, d, de), lambda j, be, nv, nu: (be[cur(j)], 0, 0)),
            pl.BlockSpec((1, de, d), lambda j, be, nv, nu: (be[cur(j)], 0, 0)),
        ],
        out_specs=pl.BlockSpec(memory_space=pl.ANY),
        scratch_shapes=[pltpu.VMEM((2, bm, d), F32),
                        pltpu.VMEM((2, bm, d), F32),
                        pltpu.SemaphoreType.DMA((2,)),
                        pltpu.SemaphoreType.DMA((2,))],
    )
    return pl.pallas_call(
        _experts_kernel,
        out_shape=jax.ShapeDtypeStruct((TOP_K * t, d), F32),
        grid_spec=grid_spec,
        compiler_params=_params(("arbitrary",)),
        name="routed_experts",
    )(block_expert, n_valid, n_used, tok_buf, slot_buf, h2, w_gate, w_up, w_down)


def _shared_kernel(h_ref, wg_ref, wu_ref, wd_ref, o_ref):
    x = h_ref[...].astype(BF16)
    hid = _silu(jnp.dot(x, wg_ref[...], preferred_element_type=F32)) * jnp.dot(
        x, wu_ref[...], preferred_element_type=F32)
    o_ref[...] = jnp.dot(hid.astype(BF16), wd_ref[...], preferred_element_type=F32).astype(o_ref.dtype)


def _shared(h2, wg, wu, wd):
    t, d = h2.shape
    ds = wg.shape[1]
    tm = _pick(t, 512, 16)
    return pl.pallas_call(
        _shared_kernel,
        out_shape=jax.ShapeDtypeStruct((t, d), F32),
        grid=(t // tm,),
        in_specs=[pl.BlockSpec((tm, d), lambda i: (i, 0)),
                  pl.BlockSpec((d, ds), lambda i: (0, 0)),
                  pl.BlockSpec((d, ds), lambda i: (0, 0)),
                  pl.BlockSpec((ds, d), lambda i: (0, 0))],
        out_specs=pl.BlockSpec((tm, d), lambda i: (i, 0)),
        compiler_params=_params(("parallel",)),
        name="shared_expert",
    )(h2, wg, wu, wd)


def _combine_kernel(x_ref, yk_ref, wt_ref, sh_ref, mod_ref, fw_ref, o_ref, *, gate_idx):
    wt = wt_ref[0]
    acc = sh_ref[0]
    for kk in range(TOP_K):
        acc = acc + wt[:, kk:kk + 1] * yk_ref[kk, 0]
    m = mod_ref[0]
    x = x_ref[0] + m[gate_idx:gate_idx + 1] * acc
    o_ref[0] = x * lax.rsqrt(jnp.mean(x * x, axis=-1, keepdims=True) + EPS) * fw_ref[...]


def _combine(x1, yk, wtok, shared, mod3, final_w):
    b, n, d = x1.shape
    tm = _pick(n, 128, 16)
    return pl.pallas_call(
        functools.partial(_combine_kernel, gate_idx=5),
        out_shape=jax.ShapeDtypeStruct((b, n, d), F32),
        grid=(b, n // tm),
        in_specs=[pl.BlockSpec((1, tm, d), lambda i, j: (i, j, 0)),
                  pl.BlockSpec((TOP_K, 1, tm, d), lambda i, j: (0, i, j, 0)),
                  pl.BlockSpec((1, tm, TOP_K), lambda i, j: (i, j, 0)),
                  pl.BlockSpec((1, tm, d), lambda i, j: (i, j, 0)),
                  pl.BlockSpec((1, 6, d), lambda i, j: (i, 0, 0)),
                  pl.BlockSpec((1, d), lambda i, j: (0, 0))],
        out_specs=pl.BlockSpec((1, tm, d), lambda i, j: (i, j, 0)),
        compiler_params=_params(("parallel", "parallel")),
        name="combine_final_norm",
    )(x1, yk, wtok, shared, mod3, final_w)


def _routing_tables(idx_kt, n_tokens):
    bm = EXPERT_BLOCK_ROWS
    n_assign = TOP_K * n_tokens
    n_blocks = n_assign // bm + N_EXPERTS
    e_flat = idx_kt.T.reshape(n_assign)
    order = jnp.argsort(e_flat).astype(I32)
    e_sorted = e_flat[order]
    bounds = jnp.searchsorted(e_sorted, jnp.arange(N_EXPERTS + 1, dtype=I32), side="left").astype(I32)
    start, counts = bounds[:-1], bounds[1:] - bounds[:-1]
    padded = (counts + bm - 1) // bm * bm
    padded_end = jnp.cumsum(padded)
    padded_start = padded_end - padded
    n_used = (padded_end[-1] // bm).astype(I32)
    blk = jnp.arange(n_blocks, dtype=I32)
    block_expert = jnp.searchsorted(padded_end, jnp.minimum(blk, n_used - 1) * bm, side="right").astype(I32)
    block_expert = jnp.minimum(block_expert, N_EXPERTS - 1)
    rows = jnp.arange(n_blocks * bm, dtype=I32)
    e_row = block_expert[rows // bm]
    j = rows - padded_start[e_row]
    valid = (j < counts[e_row]) & (rows < n_used * bm)
    f = order[jnp.clip(start[e_row] + j, 0, n_assign - 1)]
    tok = jnp.where(valid, f // TOP_K, 0)
    slot = jnp.where(valid, (f % TOP_K) * n_tokens + f // TOP_K, 0)
    n_valid = jnp.sum(valid.reshape(n_blocks, bm), axis=1).astype(I32)
    return (block_expert, n_valid, n_used.reshape(1), tok.reshape(n_blocks, 1, bm),
            slot.reshape(n_blocks, 1, bm))


def kernel(x, c, ctx, c_ctx, w_ada, b_ada, norm_mix, norm_ffn, w_in, conv_a, conv_qkv, a_log, dt_bias,
           gdn_norm, w_proj_a, w_proj_b, w_out, w_router, router_bias, w_gate, w_up, w_down,
           ws_gate, ws_up, ws_down, final_norm):
    assert w_ada.shape[0] == 1, "single (final) layer only"
    b, seq, d = x.shape
    n_ctx = ctx.shape[1]
    t = b * seq
    dc = conv_a.shape[2]
    assert seq % GRID_W == 0 and seq % CHUNK == 0 and n_ctx % CHUNK == 0
    z_off = QKV_DIM + N_GATE_COLS
    conv_off = z_off + V_DIM
    gate_off = conv_off + 3 * dc
    assert w_in.shape[2] == gate_off + 2 * d

    rows = -(-(b + 1) // 8) * 8
    cpad = jnp.zeros((rows, d), F32).at[:b].set(c).at[b].set(c_ctx)
    mod3 = _ada(cpad, w_ada[0], b_ada).reshape(rows, 6, d)

    h = _norm_mod(x, norm_mix, mod3, lambda i: i, 0, 1, BF16).reshape(t, d)
    hc = _norm_mod(ctx, norm_mix, mod3, lambda i: b, 0, 1, BF16).reshape(b * n_ctx, d)
    w_in0 = w_in[0]
    w_qkv = w_in0[:, :QKV_DIM].astype(BF16)
    w_ab = w_in0[:, QKV_DIM:z_off].astype(BF16)
    p_qkv = _mm(h, w_qkv, BF16, "inproj_qkv").reshape(b, seq, QKV_DIM)
    p_ab = _mm(h, w_ab, F32, "inproj_ab").reshape(b, seq, N_GATE_COLS)
    p_z = _mm(h, w_in0[:, z_off:conv_off].astype(BF16), BF16, "inproj_z").reshape(b, seq, V_DIM)
    p_conv = _mm(h, w_in0[:, conv_off:gate_off].astype(BF16), BF16, "inproj_conv")
    p_gate = _mm(h, w_in0[:, gate_off:].astype(BF16), BF16, "inproj_gate").reshape(b, seq, 2 * d)
    pc_qkv = _mm(hc, w_qkv, BF16, "inproj_qkv_ctx").reshape(b, n_ctx, QKV_DIM)
    pc_ab = _mm(hc, w_ab, F32, "inproj_ab_ctx").reshape(b, n_ctx, N_GATE_COLS)

    zero = jnp.zeros((N_V_HEADS,), F32)
    prm = jnp.stack([jnp.concatenate([a_log[0, 0], zero, a_log[0, 1], zero]),
                     jnp.concatenate([dt_bias[0, 0], zero, dt_bias[0, 1], zero])])
    gcol, grow = _gates(p_ab, prm)
    gcol_c, grow_c = _gates(pc_ab, prm)
    yb_in = _gdn(p_qkv, p_z, gcol, grow, pc_qkv, gcol_c, grow_c, conv_qkv[0], gdn_norm)

    ya_in = _conva(p_conv, conv_a[0])
    ya = _mm(ya_in, w_proj_a[0].astype(BF16), BF16, "proj_a").reshape(b, seq, d)
    yb = _mm(yb_in.reshape(t, V_DIM), w_proj_b[0].astype(BF16), BF16, "proj_b").reshape(b, seq, d)
    x1 = _mixout(ya, yb, p_gate, w_out[0].astype(BF16), x, mod3)

    h2, idx_bkt, wgt_bkt = _router(x1, norm_ffn, mod3, w_router[0].T, router_bias[0].reshape(N_EXPERTS, 1))
    h2 = h2.reshape(t, d)
    idx_kt = jnp.transpose(idx_bkt, (1, 0, 2)).reshape(TOP_K, t)
    wtok = jnp.transpose(wgt_bkt, (0, 2, 1))
    block_expert, n_valid, n_used, tok_buf, slot_buf = _routing_tables(idx_kt, t)
    yk = _experts(h2, block_expert, n_valid, n_used, tok_buf, slot_buf, w_gate[0], w_up[0], w_down[0])
    yk = yk.reshape(TOP_K, b, seq, d)
    shared = _shared(h2, ws_gate[0].astype(BF16), ws_up[0].astype(BF16), ws_down[0].astype(BF16)).reshape(b, seq, d)
    return _combine(x1, yk, wtok, shared, mod3, final_norm.reshape(1, d))
```

```python
import functools

import jax
import jax.numpy as jnp
from jax import lax
from jax.experimental import pallas as pl
from jax.experimental.pallas import tpu as pltpu

F32 = jnp.float32
BF16 = jnp.bfloat16
I32 = jnp.int32

EPS = 1e-6
GRID_W = 64
N_QK_HEADS = 16
N_V_HEADS = 32
HEAD_K = 128
HEAD_V = 128
CHUNK = 64
N_EXPERTS = 256
N_GROUPS = 8
TOPK_GROUPS = 4
TOP_K = 8
ROUTED_SCALE = 2.5

QK_DIM = N_QK_HEADS * HEAD_K
V_DIM = N_V_HEADS * HEAD_V
QKV_DIM = 2 * QK_DIM + V_DIM
N_GATE_COLS = 4 * N_V_HEADS

V7X_LANES = 128
V7X_VMEM_LIMIT_BYTES = 56 * 1024 * 1024
V7X_SUBLANES = 8
EXPERT_BLOCK_ROWS = 256
EXPERT_ROW_GROUP = V7X_SUBLANES
GDN_LOCAL_GROUP = 4


def _pick(n, target, mult):
    if n <= target:
        return n
    t = (target // mult) * mult
    while t >= mult:
        if n % t == 0:
            return t
        t -= mult
    raise ValueError(f"no tile for {n} (target {target}, multiple {mult})")


def _params(sem, vmem=V7X_VMEM_LIMIT_BYTES):
    return pltpu.CompilerParams(dimension_semantics=sem, vmem_limit_bytes=vmem)


def _sigmoid(x):
    return 1.0 / (1.0 + jnp.exp(-x))


def _silu(x):
    return x * _sigmoid(x)


def _dot(a, b):
    return jnp.dot(a.astype(BF16), b.astype(BF16), preferred_element_type=F32)


def _dot_nt(a, b):
    return lax.dot_general(a.astype(BF16), b.astype(BF16), (((1,), (1,)), ((), ())),
                           preferred_element_type=F32)


def _dot_tn(a, b):
    return lax.dot_general(a.astype(BF16), b.astype(BF16), (((0,), (0,)), ((), ())),
                           preferred_element_type=F32)


def _split3(x):
    hi = x.astype(BF16)
    r = x - hi.astype(F32)
    mid = r.astype(BF16)
    lo = (r - mid.astype(F32)).astype(BF16)
    return hi, mid, lo


def _ada_kernel(c_ref, w_ref, b_ref, o_ref):
    a = _silu(c_ref[...])
    o_ref[...] = _dot(a, w_ref[...]) + b_ref[...]


def _ada(cpad, w, b):
    rows, d = cpad.shape
    n = w.shape[1]
    tn = _pick(n, 1024, V7X_LANES)
    return pl.pallas_call(
        _ada_kernel,
        out_shape=jax.ShapeDtypeStruct((rows, n), F32),
        grid=(n // tn,),
        in_specs=[pl.BlockSpec((rows, d), lambda j: (0, 0)),
                  pl.BlockSpec((d, tn), lambda j: (0, j)),
                  pl.BlockSpec((1, tn), lambda j: (0, j))],
        out_specs=pl.BlockSpec((rows, tn), lambda j: (0, j)),
        compiler_params=_params(("parallel",)),
        name="ada_mod",
    )(cpad, w, b)


def _norm_mod_kernel(x_ref, w_ref, mod_ref, o_ref, *, shift_idx, scale_idx):
    x = x_ref[0]
    y = x * lax.rsqrt(jnp.mean(x * x, axis=-1, keepdims=True) + EPS) * w_ref[...]
    m = mod_ref[0]
    o_ref[0] = (y * (1.0 + m[scale_idx:scale_idx + 1]) + m[shift_idx:shift_idx + 1]).astype(o_ref.dtype)


def _norm_mod(x, w, mod3, row_of_batch, shift_idx, scale_idx, out_dtype):
    b, n, d = x.shape
    ts = _pick(n, 512, 16)
    return pl.pallas_call(
        functools.partial(_norm_mod_kernel, shift_idx=shift_idx, scale_idx=scale_idx),
        out_shape=jax.ShapeDtypeStruct((b, n, d), out_dtype),
        grid=(b, n // ts),
        in_specs=[pl.BlockSpec((1, ts, d), lambda i, j: (i, j, 0)),
                  pl.BlockSpec((1, d), lambda i, j: (0, 0)),
                  pl.BlockSpec((1, 6, d), lambda i, j: (row_of_batch(i), 0, 0))],
        out_specs=pl.BlockSpec((1, ts, d), lambda i, j: (i, j, 0)),
        compiler_params=_params(("parallel", "parallel")),
        name="norm_mod",
    )(x, w, mod3)


def _mm_kernel(a_ref, w_ref, o_ref):
    o_ref[...] = jnp.dot(a_ref[...], w_ref[...], preferred_element_type=F32).astype(o_ref.dtype)


def _mm(a, w, out_dtype, name):
    m, k = a.shape
    n = w.shape[1]
    tm = _pick(m, 1024, 16)
    tn = _pick(n, 1024, V7X_LANES)
    return pl.pallas_call(
        _mm_kernel,
        out_shape=jax.ShapeDtypeStruct((m, n), out_dtype),
        grid=(n // tn, m // tm),
        in_specs=[pl.BlockSpec((tm, k), lambda j, i: (i, 0)),
                  pl.BlockSpec((k, tn), lambda j, i: (0, j))],
        out_specs=pl.BlockSpec((tm, tn), lambda j, i: (i, j)),
        compiler_params=_params(("parallel", "parallel")),
        name=name,
    )(a, w)


def _gates_kernel(ab_ref, prm_ref, col_ref, row_ref, *, rows):
    sub = 2 * CHUNK
    n_sub = rows // sub
    prm = prm_ref[...]
    alog = prm[0:1]
    dtb = prm[1:2]
    lane = lax.broadcasted_iota(I32, (sub, N_GATE_COLS), 1)
    is_g = (lane // N_V_HEADS) % 2 == 0
    is_bwd = lane >= 2 * N_V_HEADS
    r = lax.broadcasted_iota(I32, (sub, sub), 0)
    c = lax.broadcasted_iota(I32, (sub, sub), 1)
    same = (r // CHUNK) == (c // CHUNK)
    cum_f = jnp.where(same & (c <= r), 1.0, 0.0).astype(BF16)
    cum_b = jnp.where(same & (c >= r), 1.0, 0.0).astype(BF16)
    for s in range(n_sub):
        x = ab_ref[0, s * sub:(s + 1) * sub, :]
        xs = x + dtb
        softplus = jnp.maximum(xs, 0.0) + jnp.log(1.0 + jnp.exp(-jnp.abs(xs)))
        g = -jnp.exp(alog) * softplus
        beta = _sigmoid(x)
        hi, mid, lo = _split3(g)
        pf = (jnp.dot(cum_f, hi, preferred_element_type=F32) + jnp.dot(cum_f, mid, preferred_element_type=F32)
              + jnp.dot(cum_f, lo, preferred_element_type=F32))
        pb = (jnp.dot(cum_b, hi, preferred_element_type=F32) + jnp.dot(cum_b, mid, preferred_element_type=F32)
              + jnp.dot(cum_b, lo, preferred_element_type=F32))
        out = jnp.where(is_g, jnp.where(is_bwd, pb, pf), beta)
        col_ref[0, s * sub:(s + 1) * sub, :] = out
        for h in range(2):
            oc = out[h * CHUNK:(h + 1) * CHUNK]
            row_ref[0, 2 * s + h] = jnp.concatenate([oc, oc], axis=0).T


def _gates(ab, prm):
    b, n, g = ab.shape
    rows = _pick(n, 512, 2 * CHUNK)
    return pl.pallas_call(
        functools.partial(_gates_kernel, rows=rows),
        out_shape=(jax.ShapeDtypeStruct((b, n, g), F32),
                   jax.ShapeDtypeStruct((b, n // CHUNK, g, 2 * CHUNK), F32)),
        grid=(b, n // rows),
        in_specs=[pl.BlockSpec((1, rows, g), lambda i, j: (i, j, 0)),
                  pl.BlockSpec((2, g), lambda i, j: (0, 0))],
        out_specs=(pl.BlockSpec((1, rows, g), lambda i, j: (i, j, 0)),
                   pl.BlockSpec((1, rows // CHUNK, g, 2 * CHUNK), lambda i, j: (i, j, 0, 0))),
        compiler_params=_params(("parallel", "parallel")),
        name="gate_prep",
    )(ab, prm)


def _conv_silu_tile(src_ref, w, base, rows, n_total):
    x = src_ref[0, pl.ds(base, rows), :].astype(F32)
    pb = pl.multiple_of(jnp.maximum(base - 16, 0), 16)
    nb = pl.multiple_of(jnp.minimum(base + rows, n_total - 16), 16)
    prev = src_ref[0, pl.ds(pb, 16), :].astype(F32)[15:16]
    nxt = src_ref[0, pl.ds(nb, 16), :].astype(F32)[0:1]
    prev = jnp.where(base > 0, prev, 0.0)
    nxt = jnp.where(base + rows < n_total, nxt, 0.0)
    ridx = lax.broadcasted_iota(I32, x.shape, 0)
    xp = jnp.where(ridx == 0, prev, pltpu.roll(x, 1, 0))
    xn = jnp.where(ridx == rows - 1, nxt, pltpu.roll(x, rows - 1, 0))
    y = xp * w[0:1] + x * w[1:2] + xn * w[2:3]
    return _silu(y)


def _l2norm_rows(t):
    return t * lax.rsqrt(jnp.sum(t * t, axis=-1, keepdims=True) + EPS)


def _gdn_kernel(q_ref, k_ref, v_ref, z_ref, gc_ref, gt_ref,
                qc_ref, kc_ref, vc_ref, gcc_ref, gtc_ref,
                cq_ref, ck_ref, cv_ref, gn_ref,
                y_ref,
                qs, ks, vs, osum, st, u_s, w_s, qk_s, *, n_lat, n_ctx):
    hq = pl.program_id(1)
    rep = N_V_HEADS // N_QK_HEADS
    wq = cq_ref[...]
    wk = ck_ref[...]
    wv = cv_ref[...]

    def prep(src_q, src_k, src_v, n_total, row0):
        tile = _pick(n_total, 256, 16)

        def body(t, carry):
            base = pl.multiple_of(t * tile, tile)
            dst = pl.multiple_of(row0 + base, 16)
            qs[pl.ds(dst, tile), :] = _l2norm_rows(_conv_silu_tile(src_q, wq, base, tile, n_total)) * (HEAD_K ** -0.5)
            ks[pl.ds(dst, tile), :] = _l2norm_rows(_conv_silu_tile(src_k, wk, base, tile, n_total))
            vs[pl.ds(dst, tile), :] = _conv_silu_tile(src_v, wv, base, tile, n_total)
            return carry

        lax.fori_loop(0, n_total // tile, body, 0)

    prep(qc_ref, kc_ref, vc_ref, n_ctx, 0)
    prep(q_ref, k_ref, v_ref, n_lat, n_ctx)

    st[...] = jnp.zeros(st.shape, F32)
    osum[...] = jnp.zeros(osum.shape, F32)

    wide = (CHUNK, 2 * CHUNK)
    ri = lax.broadcasted_iota(I32, wide, 0)
    li = lax.broadcasted_iota(I32, wide, 1)
    ci = li % CHUNK
    left = li < CHUNK
    eye_left = jnp.where(left & (ri == ci), 1.0, 0.0)
    incl = (ri >= ci, ri <= ci)
    strict = (ri > ci, ri < ci)
    last = (CHUNK - 1, 0)
    lane = lax.broadcasted_iota(I32, (CHUNK, N_GATE_COLS), 1)
    zeros_sq = jnp.zeros(wide, BF16)
    zeros_uw = jnp.zeros((CHUNK, 2 * HEAD_V), BF16)

    def gate_col(gtile, col):
        return jnp.sum(jnp.where(lane == col, gtile, 0.0), axis=1, keepdims=True)

    def decay_of(gcol, grow2, dirn):
        return jnp.where(incl[dirn], jnp.exp(jnp.where(incl[dirn], gcol - grow2, 0.0)), 0.0)

    def mm(a, b):
        return jnp.dot(a, b, preferred_element_type=F32)

    def local_group(c0, gcol_ref, grow_ref, row0):
        sbs, rhss, dests = [], [], []
        for dc in range(GDN_LOCAL_GROUP):
            c = c0 + dc
            r = pl.multiple_of(row0 + c * CHUNK, CHUNK)
            q = qs[pl.ds(r, CHUNK), :]
            k = ks[pl.ds(r, CHUNK), :]
            vv = vs[pl.ds(r, CHUNK), :]
            k16 = k.astype(BF16)
            kcat = jnp.concatenate([k16, k16], axis=0)
            kk2 = _dot_nt(k16, kcat)
            qk_s[pl.ds(r, CHUNK), :] = _dot_nt(q, kcat).astype(BF16)
            gtile = gcol_ref[0, pl.ds(pl.multiple_of(c * CHUNK, CHUNK), CHUNK), :]
            for dirn in range(2):
                for head in range(rep):
                    col_g = (2 * dirn) * N_V_HEADS + hq * rep + head
                    gcol = gate_col(gtile, col_g)
                    bcol = gate_col(gtile, col_g + N_V_HEADS)
                    grow2 = grow_ref[0, c, 2 * dirn, pl.ds(head, 1), :]
                    a2 = bcol * kk2 * decay_of(gcol, grow2, dirn)
                    sbs.append(jnp.where(left, eye_left, jnp.where(strict[dirn], -a2, 0.0)))
                    v = vv[:, head * HEAD_V:(head + 1) * HEAD_V]
                    rhss.append(jnp.concatenate([v * bcol, k * (bcol * jnp.exp(gcol))], axis=1).astype(BF16))
                    dests.append((dirn * rep + head, r))
        for _ in range(5):
            nxt = []
            for sb in sbs:
                sb16 = sb.astype(BF16)
                nxt.append(mm(sb16, jnp.concatenate([zeros_sq, sb16], axis=0)) + jnp.where(left, sb, 0.0))
            sbs = nxt
        sb16s = [sb.astype(BF16) for sb in sbs]
        xs = [mm(sb16, jnp.concatenate([rhs, zeros_uw], axis=0)) for sb16, rhs in zip(sb16s, rhss)]
        for sb16, x, (chain, r) in zip(sb16s, xs, dests):
            uw = x + mm(sb16, jnp.concatenate([zeros_uw, x.astype(BF16)], axis=0))
            u_s[chain, pl.ds(r, CHUNK), :] = uw[:, :HEAD_V]
            w_s[chain, pl.ds(r, CHUNK), :] = uw[:, HEAD_V:].astype(BF16)

    def seq_step(cf, cb, gcol_ref, grow_ref, row0, write_out):
        work = []
        for dirn, c in ((0, cf), (1, cb)):
            r = pl.multiple_of(row0 + c * CHUNK, CHUNK)
            lr = pl.multiple_of(c * CHUNK, CHUNK)
            k = ks[pl.ds(r, CHUNK), :]
            gtile = gcol_ref[0, pl.ds(lr, CHUNK), :]
            if write_out:
                q = qs[pl.ds(r, CHUNK), :]
                qk2 = qk_s[pl.ds(r, CHUNK), :].astype(F32)
            for head in range(rep):
                chain = dirn * rep + head
                gcol = gate_col(gtile, (2 * dirn) * N_V_HEADS + hq * rep + head)
                glast = gcol[last[dirn]:last[dirn] + 1, :]
                kdec16 = (k * jnp.exp(glast - gcol)).astype(BF16)
                lhs = w_s[chain, pl.ds(r, CHUNK), :]
                p16 = None
                if write_out:
                    grow2 = grow_ref[0, c, 2 * dirn, pl.ds(head, 1), :]
                    p16 = (qk2 * decay_of(gcol, grow2, dirn)).astype(BF16)
                    lhs = jnp.concatenate([lhs, (q * jnp.exp(gcol)).astype(BF16)], axis=0)
                work.append((chain, r, lr, head, glast, kdec16, lhs, p16))
        s_olds = [st[wk[0]] for wk in work]
        tbs = [mm(wk[6], s_old.astype(BF16)) for wk, s_old in zip(work, s_olds)]
        for (chain, r, lr, head, glast, kdec16, _, p16), s_old, tb in zip(work, s_olds, tbs):
            vn16 = (u_s[chain, pl.ds(r, CHUNK), :] - tb[:CHUNK]).astype(BF16)
            if write_out:
                o = tb[CHUNK:] + mm(p16, jnp.concatenate([vn16, zeros_sq], axis=0))
                osum[pl.ds(lr, CHUNK), head * HEAD_V:(head + 1) * HEAD_V] += o
            st[chain] = s_old * jnp.exp(glast) + _dot_tn(kdec16, vn16)

    def scan(n_chunks, gcol_ref, grow_ref, row0, write_out):
        def local_body(i, carry):
            local_group(GDN_LOCAL_GROUP * i, gcol_ref, grow_ref, row0)
            return carry

        def seq_body(n, carry):
            seq_step(n, n_chunks - 1 - n, gcol_ref, grow_ref, row0, write_out)
            return carry

        lax.fori_loop(0, n_chunks // GDN_LOCAL_GROUP, local_body, 0)
        lax.fori_loop(0, n_chunks, seq_body, 0)

    scan(n_ctx // CHUNK, gcc_ref, gtc_ref, 0, False)
    scan(n_lat // CHUNK, gc_ref, gt_ref, n_ctx, True)

    gn = gn_ref[...]
    tile = _pick(n_lat, 256, 16)

    def epi(t, carry):
        base = pl.multiple_of(t * tile, tile)
        o = osum[pl.ds(base, tile), :]
        z = z_ref[0, pl.ds(base, tile), :].astype(F32)
        for head in range(rep):
            sl = slice(head * HEAD_V, (head + 1) * HEAD_V)
            oh = o[:, sl]
            y = oh * lax.rsqrt(jnp.mean(oh * oh, axis=-1, keepdims=True) + EPS) * gn * _silu(z[:, sl])
            y_ref[0, pl.ds(base, tile), sl] = y.astype(y_ref.dtype)
        return carry

    lax.fori_loop(0, n_lat // tile, epi, 0)


def _gdn(p_qkv, p_z, gcol, grow, pc_qkv, gcol_c, grow_c, conv_qkv, gdn_norm):
    b, n_lat, _ = p_qkv.shape
    n_ctx = pc_qkv.shape[1]
    rep = N_V_HEADS // N_QK_HEADS
    vw = rep * HEAD_V
    kq0 = QK_DIM // HEAD_K
    v0 = 2 * QK_DIM // vw
    nc_lat, nc_ctx = n_lat // CHUNK, n_ctx // CHUNK
    assert nc_lat % GDN_LOCAL_GROUP == 0 and nc_ctx % GDN_LOCAL_GROUP == 0
    assert HEAD_K == HEAD_V == N_GATE_COLS == 2 * CHUNK
    grow5 = grow.reshape(b, nc_lat, 4, N_QK_HEADS, rep, 2 * CHUNK)
    grow5_c = grow_c.reshape(b, nc_ctx, 4, N_QK_HEADS, rep, 2 * CHUNK)

    def seq_specs(n):
        return [pl.BlockSpec((1, n, HEAD_K), lambda i, h: (i, 0, h)),
                pl.BlockSpec((1, n, HEAD_K), lambda i, h: (i, 0, kq0 + h)),
                pl.BlockSpec((1, n, vw), lambda i, h: (i, 0, v0 + h))]

    def gate_specs(n, nc):
        return [pl.BlockSpec((1, n, N_GATE_COLS), lambda i, h: (i, 0, 0)),
                pl.BlockSpec((1, nc, 4, None, rep, 2 * CHUNK), lambda i, h: (i, 0, 0, h, 0, 0))]

    in_specs = (seq_specs(n_lat)
                + [pl.BlockSpec((1, n_lat, vw), lambda i, h: (i, 0, h))]
                + gate_specs(n_lat, nc_lat)
                + seq_specs(n_ctx)
                + gate_specs(n_ctx, nc_ctx)
                + [pl.BlockSpec((3, HEAD_K), lambda i, h: (0, h)),
                   pl.BlockSpec((3, HEAD_K), lambda i, h: (0, kq0 + h)),
                   pl.BlockSpec((3, vw), lambda i, h: (0, v0 + h)),
                   pl.BlockSpec((1, HEAD_V), lambda i, h: (0, 0))])
    n_all = n_ctx + n_lat
    return pl.pallas_call(
        functools.partial(_gdn_kernel, n_lat=n_lat, n_ctx=n_ctx),
        out_shape=jax.ShapeDtypeStruct((b, n_lat, V_DIM), BF16),
        grid=(b, N_QK_HEADS),
        in_specs=in_specs,
        out_specs=pl.BlockSpec((1, n_lat, vw), lambda i, h: (i, 0, h)),
        scratch_shapes=[pltpu.VMEM((n_all, HEAD_K), F32),
                        pltpu.VMEM((n_all, HEAD_K), F32),
                        pltpu.VMEM((n_all, vw), F32),
                        pltpu.VMEM((n_lat, vw), F32),
                        pltpu.VMEM((2 * rep, HEAD_K, HEAD_V), F32),
                        pltpu.VMEM((2 * rep, n_all, HEAD_V), F32),
                        pltpu.VMEM((2 * rep, n_all, HEAD_K), BF16),
                        pltpu.VMEM((n_all, 2 * CHUNK), BF16)],
        compiler_params=_params(("parallel", "parallel")),
        name="gdn_bidir",
    )(p_qkv, p_qkv, p_qkv, p_z, gcol, grow5,
      pc_qkv, pc_qkv, pc_qkv, gcol_c, grow5_c,
      conv_qkv, conv_qkv, conv_qkv, gdn_norm)


def _conva_kernel(gb_ref, gcx_ref, xin_ref, w_ref, o_ref):
    u = gcx_ref[...].astype(F32) * xin_ref[...].astype(F32)
    rows = u.shape[0]
    pos = lax.broadcasted_iota(I32, u.shape, 0) % GRID_W
    up = jnp.where(pos == 0, 0.0, pltpu.roll(u, 1, 0))
    un = jnp.where(pos == GRID_W - 1, 0.0, pltpu.roll(u, rows - 1, 0))
    w = w_ref[...]
    conv = up * w[0:1] + u * w[1:2] + un * w[2:3]
    o_ref[...] = (gb_ref[...].astype(F32) * conv).astype(o_ref.dtype)


def _conva(p_conv, conv_w):
    t, c3 = p_conv.shape
    dc = c3 // 3
    tr = _pick(t, 512, GRID_W)
    tc = _pick(dc, 512, V7X_LANES)
    nj = dc // tc
    return pl.pallas_call(
        _conva_kernel,
        out_shape=jax.ShapeDtypeStruct((t, dc), BF16),
        grid=(t // tr, nj),
        in_specs=[pl.BlockSpec((tr, tc), lambda i, j: (i, j)),
                  pl.BlockSpec((tr, tc), lambda i, j: (i, nj + j)),
                  pl.BlockSpec((tr, tc), lambda i, j: (i, 2 * nj + j)),
                  pl.BlockSpec((3, tc), lambda i, j: (0, j))],
        out_specs=pl.BlockSpec((tr, tc), lambda i, j: (i, j)),
        compiler_params=_params(("parallel", "parallel")),
        name="mixer_a_conv",
    )(p_conv, p_conv, p_conv, conv_w)


def _mixout_kernel(ya_ref, yb_ref, ga_ref, gb_ref, wo_ref, x_ref, mod_ref, o_ref, *, gate_idx):
    mix = (_sigmoid(ga_ref[0].astype(F32)) * ya_ref[0].astype(F32)
           + _sigmoid(gb_ref[0].astype(F32)) * yb_ref[0].astype(F32))
    out = _dot(mix, wo_ref[...])
    m = mod_ref[0]
    o_ref[0] = x_ref[0] + m[gate_idx:gate_idx + 1] * out


def _mixout(ya, yb, p_gate, w_o, x, mod3):
    b, n, d = x.shape
    tm = _pick(n, 512, 16)
    return pl.pallas_call(
        functools.partial(_mixout_kernel, gate_idx=2),
        out_shape=jax.ShapeDtypeStruct((b, n, d), F32),
        grid=(b, n // tm),
        in_specs=[pl.BlockSpec((1, tm, d), lambda i, j: (i, j, 0)),
                  pl.BlockSpec((1, tm, d), lambda i, j: (i, j, 0)),
                  pl.BlockSpec((1, tm, d), lambda i, j: (i, j, 0)),
                  pl.BlockSpec((1, tm, d), lambda i, j: (i, j, 1)),
                  pl.BlockSpec((d, d), lambda i, j: (0, 0)),
                  pl.BlockSpec((1, tm, d), lambda i, j: (i, j, 0)),
                  pl.BlockSpec((1, 6, d), lambda i, j: (i, 0, 0))],
        out_specs=pl.BlockSpec((1, tm, d), lambda i, j: (i, j, 0)),
        compiler_params=_params(("parallel", "parallel")),
        name="merge_outproj",
    )(ya, yb, p_gate, p_gate, w_o, x, mod3)


def _router_kernel(x_ref, w_ref, mod_ref, wr_ref, bias_ref, h_ref, idx_ref, wgt_ref, *, shift_idx, scale_idx):
    x = x_ref[0]
    y = x * lax.rsqrt(jnp.mean(x * x, axis=-1, keepdims=True) + EPS) * w_ref[...]
    m = mod_ref[0]
    h = y * (1.0 + m[scale_idx:scale_idx + 1]) + m[shift_idx:shift_idx + 1]
    h_ref[0] = h
    tm = h.shape[0]
    h_hi = h.astype(BF16)
    h_lo = (h - h_hi.astype(F32)).astype(BF16)
    wr = wr_ref[...]
    w_hi = wr.astype(BF16)
    w_lo = (wr - w_hi.astype(F32)).astype(BF16)
    logits = _dot_nt(w_hi, h_hi) + _dot_nt(w_hi, h_lo) + _dot_nt(w_lo, h_hi)
    scores = _sigmoid(logits)
    biased = scores + bias_ref[...]
    per_group = N_EXPERTS // N_GROUPS
    neg = -jnp.inf
    i32g = lax.broadcasted_iota(I32, (per_group, tm), 0).astype(F32)
    groups, gscore = [], []
    for g in range(N_GROUPS):
        sg = biased[g * per_group:(g + 1) * per_group, :]
        m1 = jnp.max(sg, axis=0, keepdims=True)
        a1 = jnp.min(jnp.where(sg == m1, i32g, float(per_group)), axis=0, keepdims=True)
        m2 = jnp.max(jnp.where(i32g == a1, neg, sg), axis=0, keepdims=True)
        groups.append(sg)
        gscore.append(m1 + m2)
    parts = []
    for g in range(N_GROUPS):
        ahead = jnp.zeros((1, tm), F32)
        for o in range(N_GROUPS):
            if o == g:
                continue
            beats = (gscore[o] >= gscore[g]) if o < g else (gscore[o] > gscore[g])
            ahead = ahead + jnp.where(beats, 1.0, 0.0)
        parts.append(jnp.where(ahead < float(TOPK_GROUPS), groups[g], neg))
    masked = jnp.concatenate(parts, axis=0)
    ei = lax.broadcasted_iota(I32, masked.shape, 0).astype(F32)
    idxs, wgts = [], []
    for _ in range(TOP_K):
        em = jnp.max(masked, axis=0, keepdims=True)
        ea = jnp.min(jnp.where(masked == em, ei, float(N_EXPERTS)), axis=0, keepdims=True)
        hit = ei == ea
        idxs.append(ea)
        wgts.append(jnp.sum(jnp.where(hit, scores, 0.0), axis=0, keepdims=True))
        masked = jnp.where(hit, neg, masked)
    idx = jnp.concatenate(idxs, axis=0)
    wgt = jnp.concatenate(wgts, axis=0)
    wgt = wgt / jnp.sum(wgt, axis=0, keepdims=True) * ROUTED_SCALE
    idx_ref[0] = idx.astype(I32)
    wgt_ref[0] = wgt


def _router(x1, norm_w, mod3, wr_t, bias_col):
    b, n, d = x1.shape
    tm = _pick(n, 256, V7X_LANES)
    nt = n // tm
    return pl.pallas_call(
        functools.partial(_router_kernel, shift_idx=3, scale_idx=4),
        out_shape=(jax.ShapeDtypeStruct((b, n, d), F32),
                   jax.ShapeDtypeStruct((b, TOP_K, n), I32),
                   jax.ShapeDtypeStruct((b, TOP_K, n), F32)),
        grid=(b, nt),
        in_specs=[pl.BlockSpec((1, tm, d), lambda i, j: (i, j, 0)),
                  pl.BlockSpec((1, d), lambda i, j: (0, 0)),
                  pl.BlockSpec((1, 6, d), lambda i, j: (i, 0, 0)),
                  pl.BlockSpec((N_EXPERTS, d), lambda i, j: (0, 0)),
                  pl.BlockSpec((N_EXPERTS, 1), lambda i, j: (0, 0))],
        out_specs=(pl.BlockSpec((1, tm, d), lambda i, j: (i, j, 0)),
                   pl.BlockSpec((1, TOP_K, tm), lambda i, j: (i, 0, j)),
                   pl.BlockSpec((1, TOP_K, tm), lambda i, j: (i, 0, j))),
        compiler_params=_params(("parallel", "parallel")),
        name="ffn_norm_router",
    )(x1, norm_w, mod3, wr_t, bias_col)


def _experts_kernel(be_ref, nv_ref, nu_ref, tok_ref, slot_ref, h_hbm, wg_ref, wu_ref, wd_ref, y_hbm, spare_hbm,
                    xbuf, ybuf, gsem, ssem):
    j = pl.program_id(0)
    n_used = nu_ref[0]
    grp = EXPERT_ROW_GROUP

    def groups(blk):
        return lax.shift_right_logical(nv_ref[blk] + (grp - 1), grp.bit_length() - 1)

    def gather_row(blk, r):
        return pltpu.make_async_copy(h_hbm.at[pl.ds(tok_ref[0, 0, r], 1)],
                                     xbuf.at[blk % 2, pl.ds(r, 1)], gsem.at[blk % 2])

    def scatter_row(blk, r, dst):
        return pltpu.make_async_copy(ybuf.at[blk % 2, pl.ds(r, 1)], dst, ssem.at[blk % 2])

    def wait_gather(blk):
        n = pl.multiple_of(groups(blk) * grp, grp)
        pltpu.make_async_copy(h_hbm.at[pl.ds(0, n)], xbuf.at[blk % 2, pl.ds(0, n)], gsem.at[blk % 2]).wait()

    def wait_scatter(blk):
        n = pl.multiple_of(groups(blk) * grp, grp)
        pltpu.make_async_copy(ybuf.at[blk % 2, pl.ds(0, n)], y_hbm.at[pl.ds(0, n)], ssem.at[blk % 2]).wait()

    def start_gather(blk):
        def body(g, c):
            for u in range(grp):
                gather_row(blk, g * grp + u).start()
            return c
        lax.fori_loop(0, groups(blk), body, 0)

    def start_scatter(blk):
        n = nv_ref[blk]
        full = lax.shift_right_logical(n, grp.bit_length() - 1)

        def body(g, c):
            for u in range(grp):
                r = g * grp + u
                scatter_row(blk, r, y_hbm.at[pl.ds(slot_ref[0, 0, r], 1)]).start()
            return c
        lax.fori_loop(0, full, body, 0)

        @pl.when(full * grp < n)
        def _():
            for u in range(grp):
                r = full * grp + u

                @pl.when(r < n)
                def _():
                    scatter_row(blk, r, y_hbm.at[pl.ds(slot_ref[0, 0, r], 1)]).start()

                @pl.when(r >= n)
                def _():
                    scatter_row(blk, r, spare_hbm.at[pl.ds((blk % 2) * grp + u, 1)]).start()

    @pl.when(j == 0)
    def _():
        xbuf[...] = jnp.zeros(xbuf.shape, F32)
        init = pltpu.make_async_copy(xbuf.at[0, pl.ds(0, 2 * grp)], spare_hbm, gsem.at[0])
        init.start()
        init.wait()

    @pl.when(j < n_used)
    def _():
        start_gather(j)

    @pl.when((j >= 1) & (j - 1 < n_used))
    def _():
        i = j - 1
        wait_gather(i)

        @pl.when(i >= 2)
        def _():
            wait_scatter(i - 2)

        x = xbuf[i % 2].astype(BF16)
        hid = _silu(jnp.dot(x, wg_ref[0].astype(BF16), preferred_element_type=F32)) * jnp.dot(
            x, wu_ref[0].astype(BF16), preferred_element_type=F32)
        ybuf[i % 2] = jnp.dot(hid.astype(BF16), wd_ref[0].astype(BF16), preferred_element_type=F32)

        start_scatter(i)

        @pl.when(i == n_used - 1)
        def _():
            wait_scatter(i)

            @pl.when(i >= 1)
            def _():
                wait_scatter(i - 1)


def _experts(h2, block_expert, n_valid, n_used, tok_buf, slot_buf, w_gate, w_up, w_down):
    t, d = h2.shape
    de = w_gate.shape[2]
    n_blocks, _, bm = tok_buf.shape

    def cur(j):
        return jnp.maximum(j - 1, 0)

    grid_spec = pltpu.PrefetchScalarGridSpec(
        num_scalar_prefetch=3,
        grid=(n_blocks + 1,),
        in_specs=[
            pl.BlockSpec((1, 1, bm), lambda j, be, nv, nu: (jnp.minimum(j, n_blocks - 1), 0, 0),
                         memory_space=pltpu.SMEM),
            pl.BlockSpec((1, 1, bm), lambda j, be, nv, nu: (cur(j), 0, 0), memory_space=pltpu.SMEM),
            pl.BlockSpec(memory_space=pl.ANY),
            pl.BlockSpec((1, d, de), lambda j, be, nv, nu: (be[cur(j)], 0, 0)),
            pl.BlockSpec((1, d, de), lambda j, be, nv, nu: (be[cur(j)], 0, 0)),
            pl.BlockSpec((1, de, d), lambda j, be, nv, nu: (be[cur(j)], 0, 0)),
        ],
        out_specs=(pl.BlockSpec(memory_space=pl.ANY), pl.BlockSpec(memory_space=pl.ANY)),
        scratch_shapes=[pltpu.VMEM((2, bm, d), F32),
                        pltpu.VMEM((2, bm, d), F32),
                        pltpu.SemaphoreType.DMA((2,)),
                        pltpu.SemaphoreType.DMA((2,))],
    )
    assert bm % EXPERT_ROW_GROUP == 0
    yk, _ = pl.pallas_call(
        _experts_kernel,
        out_shape=(jax.ShapeDtypeStruct((TOP_K * t, d), F32),
                   jax.ShapeDtypeStruct((2 * EXPERT_ROW_GROUP, d), F32)),
        grid_spec=grid_spec,
        compiler_params=_params(("arbitrary",)),
        name="routed_experts",
    )(block_expert, n_valid, n_used, tok_buf, slot_buf, h2, w_gate, w_up, w_down)
    return yk


def _shared_kernel(h_ref, wg_ref, wu_ref, wd_ref, o_ref):
    x = h_ref[...].astype(BF16)
    hid = _silu(jnp.dot(x, wg_ref[...], preferred_element_type=F32)) * jnp.dot(
        x, wu_ref[...], preferred_element_type=F32)
    o_ref[...] = jnp.dot(hid.astype(BF16), wd_ref[...], preferred_element_type=F32).astype(o_ref.dtype)


def _shared(h2, wg, wu, wd):
    t, d = h2.shape
    ds = wg.shape[1]
    tm = _pick(t, 512, 16)
    return pl.pallas_call(
        _shared_kernel,
        out_shape=jax.ShapeDtypeStruct((t, d), F32),
        grid=(t // tm,),
        in_specs=[pl.BlockSpec((tm, d), lambda i: (i, 0)),
                  pl.BlockSpec((d, ds), lambda i: (0, 0)),
                  pl.BlockSpec((d, ds), lambda i: (0, 0)),
                  pl.BlockSpec((ds, d), lambda i: (0, 0))],
        out_specs=pl.BlockSpec((tm, d), lambda i: (i, 0)),
        compiler_params=_params(("parallel",)),
        name="shared_expert",
    )(h2, wg, wu, wd)


def _combine_kernel(x_ref, yk_ref, wt_ref, sh_ref, mod_ref, fw_ref, o_ref, *, gate_idx):
    wt = wt_ref[0]
    acc = sh_ref[0]
    for kk in range(TOP_K):
        acc = acc + wt[:, kk:kk + 1] * yk_ref[kk, 0]
    m = mod_ref[0]
    x = x_ref[0] + m[gate_idx:gate_idx + 1] * acc
    o_ref[0] = x * lax.rsqrt(jnp.mean(x * x, axis=-1, keepdims=True) + EPS) * fw_ref[...]


def _combine(x1, yk, wtok, shared, mod3, final_w):
    b, n, d = x1.shape
    tm = _pick(n, 128, 16)
    return pl.pallas_call(
        functools.partial(_combine_kernel, gate_idx=5),
        out_shape=jax.ShapeDtypeStruct((b, n, d), F32),
        grid=(b, n // tm),
        in_specs=[pl.BlockSpec((1, tm, d), lambda i, j: (i, j, 0)),
                  pl.BlockSpec((TOP_K, 1, tm, d), lambda i, j: (0, i, j, 0)),
                  pl.BlockSpec((1, tm, TOP_K), lambda i, j: (i, j, 0)),
                  pl.BlockSpec((1, tm, d), lambda i, j: (i, j, 0)),
                  pl.BlockSpec((1, 6, d), lambda i, j: (i, 0, 0)),
                  pl.BlockSpec((1, d), lambda i, j: (0, 0))],
        out_specs=pl.BlockSpec((1, tm, d), lambda i, j: (i, j, 0)),
        compiler_params=_params(("parallel", "parallel")),
        name="combine_final_norm",
    )(x1, yk, wtok, shared, mod3, final_w)


def _routing_tables(idx_kt, n_tokens):
    bm = EXPERT_BLOCK_ROWS
    n_assign = TOP_K * n_tokens
    assert n_assign % bm == 0
    n_blocks = n_assign // bm + N_EXPERTS
    e_flat = idx_kt.reshape(n_assign)
    experts = jnp.arange(N_EXPERTS, dtype=I32)
    counts = jnp.sum((e_flat[None, :] == experts[:, None]).astype(I32), axis=1)
    n_pad = (-counts) % bm
    big = 2 * N_EXPERTS
    fill = jnp.arange(bm, dtype=I32)[None, :] < n_pad[:, None]
    fill_keys = jnp.where(fill, 2 * experts[:, None] + 1, big).reshape(N_EXPERTS * bm)
    keys = jnp.concatenate([2 * e_flat, fill_keys])
    vals = jnp.concatenate([jnp.arange(n_assign, dtype=I32), jnp.full((N_EXPERTS * bm,), -1, I32)])
    keys, vals = lax.sort((keys, vals), num_keys=1)
    valid = vals >= 0
    tok = jnp.where(valid, vals % n_tokens, 0)
    slot = jnp.where(valid, vals, 0)
    n_valid = jnp.sum(valid.reshape(n_blocks, bm).astype(I32), axis=1)
    n_used = (jnp.sum(counts + n_pad) // bm).astype(I32)
    blk = jnp.minimum(jnp.arange(n_blocks, dtype=I32), n_used - 1)
    block_expert = keys.reshape(n_blocks, bm)[:, 0][blk] // 2
    return (block_expert, n_valid, n_used.reshape(1), tok.reshape(n_blocks, 1, bm),
            slot.reshape(n_blocks, 1, bm))


def kernel(x, c, ctx, c_ctx, w_ada, b_ada, norm_mix, norm_ffn, w_in, conv_a, conv_qkv, a_log, dt_bias,
           gdn_norm, w_proj_a, w_proj_b, w_out, w_router, router_bias, w_gate, w_up, w_down,
           ws_gate, ws_up, ws_down, final_norm):
    assert w_ada.shape[0] == 1, "single (final) layer only"
    b, seq, d = x.shape
    n_ctx = ctx.shape[1]
    t = b * seq
    dc = conv_a.shape[2]
    assert seq % GRID_W == 0 and seq % CHUNK == 0 and n_ctx % CHUNK == 0
    z_off = QKV_DIM + N_GATE_COLS
    conv_off = z_off + V_DIM
    gate_off = conv_off + 3 * dc
    assert w_in.shape[2] == gate_off + 2 * d

    rows = -(-(b + 1) // 8) * 8
    cpad = jnp.zeros((rows, d), F32).at[:b].set(c).at[b].set(c_ctx)
    mod3 = _ada(cpad, w_ada[0], b_ada).reshape(rows, 6, d)

    h = _norm_mod(x, norm_mix, mod3, lambda i: i, 0, 1, BF16).reshape(t, d)
    hc = _norm_mod(ctx, norm_mix, mod3, lambda i: b, 0, 1, BF16).reshape(b * n_ctx, d)
    w_in0 = w_in[0]
    w_qkv = w_in0[:, :QKV_DIM].astype(BF16)
    w_ab = w_in0[:, QKV_DIM:z_off].astype(BF16)
    p_qkv = _mm(h, w_qkv, BF16, "inproj_qkv").reshape(b, seq, QKV_DIM)
    p_ab = _mm(h, w_ab, F32, "inproj_ab").reshape(b, seq, N_GATE_COLS)
    p_z = _mm(h, w_in0[:, z_off:conv_off].astype(BF16), BF16, "inproj_z").reshape(b, seq, V_DIM)
    p_conv = _mm(h, w_in0[:, conv_off:gate_off].astype(BF16), BF16, "inproj_conv")
    p_gate = _mm(h, w_in0[:, gate_off:].astype(BF16), BF16, "inproj_gate").reshape(b, seq, 2 * d)
    pc_qkv = _mm(hc, w_qkv, BF16, "inproj_qkv_ctx").reshape(b, n_ctx, QKV_DIM)
    pc_ab = _mm(hc, w_ab, F32, "inproj_ab_ctx").reshape(b, n_ctx, N_GATE_COLS)

    zero = jnp.zeros((N_V_HEADS,), F32)
    prm = jnp.stack([jnp.concatenate([a_log[0, 0], zero, a_log[0, 1], zero]),
                     jnp.concatenate([dt_bias[0, 0], zero, dt_bias[0, 1], zero])])
    gcol, grow = _gates(p_ab, prm)
    gcol_c, grow_c = _gates(pc_ab, prm)
    yb_in = _gdn(p_qkv, p_z, gcol, grow, pc_qkv, gcol_c, grow_c, conv_qkv[0], gdn_norm)

    ya_in = _conva(p_conv, conv_a[0])
    ya = _mm(ya_in, w_proj_a[0].astype(BF16), BF16, "proj_a").reshape(b, seq, d)
    yb = _mm(yb_in.reshape(t, V_DIM), w_proj_b[0].astype(BF16), BF16, "proj_b").reshape(b, seq, d)
    x1 = _mixout(ya, yb, p_gate, w_out[0].astype(BF16), x, mod3)

    h2, idx_bkt, wgt_bkt = _router(x1, norm_ffn, mod3, w_router[0].T, router_bias[0].reshape(N_EXPERTS, 1))
    h2 = h2.reshape(t, d)
    idx_kt = jnp.transpose(idx_bkt, (1, 0, 2)).reshape(TOP_K, t)
    wtok = jnp.transpose(wgt_bkt, (0, 2, 1))
    block_expert, n_valid, n_used, tok_buf, slot_buf = _routing_tables(idx_kt, t)
    yk = _experts(h2, block_expert, n_valid, n_used, tok_buf, slot_buf, w_gate[0], w_up[0], w_down[0])
    yk = yk.reshape(TOP_K, b, seq, d)
    shared = _shared(h2, ws_gate[0].astype(BF16), ws_up[0].astype(BF16), ws_down[0].astype(BF16)).reshape(b, seq, d)
    return _combine(x1, yk, wtok, shared, mod3, final_norm.reshape(1, d))
```

```python
import functools

import jax
import jax.numpy as jnp
from jax import lax
from jax.experimental import pallas as pl
from jax.experimental.pallas import tpu as pltpu

F32 = jnp.float32
BF16 = jnp.bfloat16
I32 = jnp.int32

EPS = 1e-6
GRID_W = 64
N_QK_HEADS = 16
N_V_HEADS = 32
HEAD_K = 128
HEAD_V = 128
CHUNK = 64
N_EXPERTS = 256
N_GROUPS = 8
TOPK_GROUPS = 4
TOP_K = 8
ROUTED_SCALE = 2.5

QK_DIM = N_QK_HEADS * HEAD_K
V_DIM = N_V_HEADS * HEAD_V
QKV_DIM = 2 * QK_DIM + V_DIM
N_GATE_COLS = 4 * N_V_HEADS

V7X_LANES = 128
V7X_VMEM_LIMIT_BYTES = 56 * 1024 * 1024
V7X_MXU_COLS = 256
EXPERT_BLOCK_ROWS = 256
GDN_LOCAL_GROUP = 4


def _pick(n, target, mult):
    if n <= target:
        return n
    t = (target // mult) * mult
    while t >= mult:
        if n % t == 0:
            return t
        t -= mult
    raise ValueError(f"no tile for {n} (target {target}, multiple {mult})")


def _params(sem, vmem=V7X_VMEM_LIMIT_BYTES):
    return pltpu.CompilerParams(dimension_semantics=sem, vmem_limit_bytes=vmem)


def _sigmoid(x):
    return 1.0 / (1.0 + jnp.exp(-x))


def _silu(x):
    return x * _sigmoid(x)


def _dot(a, b):
    return jnp.dot(a.astype(BF16), b.astype(BF16), preferred_element_type=F32)


def _dot_nt(a, b):
    return lax.dot_general(a.astype(BF16), b.astype(BF16), (((1,), (1,)), ((), ())),
                           preferred_element_type=F32)


def _dot_tn(a, b):
    return lax.dot_general(a.astype(BF16), b.astype(BF16), (((0,), (0,)), ((), ())),
                           preferred_element_type=F32)


def _split3(x):
    hi = x.astype(BF16)
    r = x - hi.astype(F32)
    mid = r.astype(BF16)
    lo = (r - mid.astype(F32)).astype(BF16)
    return hi, mid, lo


def _ada_kernel(c_ref, w_ref, b_ref, o_ref):
    a = _silu(c_ref[...])
    o_ref[...] = _dot(a, w_ref[...]) + b_ref[...]


def _ada(cpad, w, b):
    rows, d = cpad.shape
    n = w.shape[1]
    tn = _pick(n, 1024, V7X_LANES)
    return pl.pallas_call(
        _ada_kernel,
        out_shape=jax.ShapeDtypeStruct((rows, n), F32),
        grid=(n // tn,),
        in_specs=[pl.BlockSpec((rows, d), lambda j: (0, 0)),
                  pl.BlockSpec((d, tn), lambda j: (0, j)),
                  pl.BlockSpec((1, tn), lambda j: (0, j))],
        out_specs=pl.BlockSpec((rows, tn), lambda j: (0, j)),
        compiler_params=_params(("parallel",)),
        name="ada_mod",
    )(cpad, w, b)


def _norm_mod_kernel(x_ref, w_ref, mod_ref, o_ref, *, shift_idx, scale_idx):
    x = x_ref[0]
    y = x * lax.rsqrt(jnp.mean(x * x, axis=-1, keepdims=True) + EPS) * w_ref[...]
    m = mod_ref[0]
    o_ref[0] = (y * (1.0 + m[scale_idx:scale_idx + 1]) + m[shift_idx:shift_idx + 1]).astype(o_ref.dtype)


def _norm_mod(x, w, mod3, row_of_batch, shift_idx, scale_idx, out_dtype):
    b, n, d = x.shape
    ts = _pick(n, 512, 16)
    return pl.pallas_call(
        functools.partial(_norm_mod_kernel, shift_idx=shift_idx, scale_idx=scale_idx),
        out_shape=jax.ShapeDtypeStruct((b, n, d), out_dtype),
        grid=(b, n // ts),
        in_specs=[pl.BlockSpec((1, ts, d), lambda i, j: (i, j, 0)),
                  pl.BlockSpec((1, d), lambda i, j: (0, 0)),
                  pl.BlockSpec((1, 6, d), lambda i, j: (row_of_batch(i), 0, 0))],
        out_specs=pl.BlockSpec((1, ts, d), lambda i, j: (i, j, 0)),
        compiler_params=_params(("parallel", "parallel")),
        name="norm_mod",
    )(x, w, mod3)


def _mm_kernel(a_ref, w_ref, o_ref):
    o_ref[...] = jnp.dot(a_ref[...], w_ref[...], preferred_element_type=F32).astype(o_ref.dtype)


def _mm(a, w, out_dtype, name):
    m, k = a.shape
    n = w.shape[1]
    tm = _pick(m, 1024, 16)
    tn = _pick(n, 1024, V7X_LANES)
    return pl.pallas_call(
        _mm_kernel,
        out_shape=jax.ShapeDtypeStruct((m, n), out_dtype),
        grid=(n // tn, m // tm),
        in_specs=[pl.BlockSpec((tm, k), lambda j, i: (i, 0)),
                  pl.BlockSpec((k, tn), lambda j, i: (0, j))],
        out_specs=pl.BlockSpec((tm, tn), lambda j, i: (i, j)),
        compiler_params=_params(("parallel", "parallel")),
        name=name,
    )(a, w)


def _gates_kernel(ab_ref, prm_ref, col_ref, row_ref, *, rows):
    sub = 2 * CHUNK
    n_sub = rows // sub
    prm = prm_ref[...]
    alog = prm[0:1]
    dtb = prm[1:2]
    lane = lax.broadcasted_iota(I32, (sub, N_GATE_COLS), 1)
    is_g = (lane // N_V_HEADS) % 2 == 0
    is_bwd = lane >= 2 * N_V_HEADS
    r = lax.broadcasted_iota(I32, (sub, sub), 0)
    c = lax.broadcasted_iota(I32, (sub, sub), 1)
    same = (r // CHUNK) == (c // CHUNK)
    cum_f = jnp.where(same & (c <= r), 1.0, 0.0).astype(BF16)
    cum_b = jnp.where(same & (c >= r), 1.0, 0.0).astype(BF16)
    for s in range(n_sub):
        x = ab_ref[0, s * sub:(s + 1) * sub, :]
        xs = x + dtb
        softplus = jnp.maximum(xs, 0.0) + jnp.log(1.0 + jnp.exp(-jnp.abs(xs)))
        g = -jnp.exp(alog) * softplus
        beta = _sigmoid(x)
        hi, mid, lo = _split3(g)
        pf = (jnp.dot(cum_f, hi, preferred_element_type=F32) + jnp.dot(cum_f, mid, preferred_element_type=F32)
              + jnp.dot(cum_f, lo, preferred_element_type=F32))
        pb = (jnp.dot(cum_b, hi, preferred_element_type=F32) + jnp.dot(cum_b, mid, preferred_element_type=F32)
              + jnp.dot(cum_b, lo, preferred_element_type=F32))
        out = jnp.where(is_g, jnp.where(is_bwd, pb, pf), beta)
        col_ref[0, s * sub:(s + 1) * sub, :] = out
        for h in range(2):
            oc = out[h * CHUNK:(h + 1) * CHUNK]
            row_ref[0, 2 * s + h] = jnp.concatenate([oc, oc], axis=0).T


def _gates(ab, prm):
    b, n, g = ab.shape
    rows = _pick(n, 512, 2 * CHUNK)
    return pl.pallas_call(
        functools.partial(_gates_kernel, rows=rows),
        out_shape=(jax.ShapeDtypeStruct((b, n, g), F32),
                   jax.ShapeDtypeStruct((b, n // CHUNK, g, 2 * CHUNK), F32)),
        grid=(b, n // rows),
        in_specs=[pl.BlockSpec((1, rows, g), lambda i, j: (i, j, 0)),
                  pl.BlockSpec((2, g), lambda i, j: (0, 0))],
        out_specs=(pl.BlockSpec((1, rows, g), lambda i, j: (i, j, 0)),
                   pl.BlockSpec((1, rows // CHUNK, g, 2 * CHUNK), lambda i, j: (i, j, 0, 0))),
        compiler_params=_params(("parallel", "parallel")),
        name="gate_prep",
    )(ab, prm)


def _conv_silu_tile(src_ref, w, base, rows, n_total):
    x = src_ref[0, pl.ds(base, rows), :].astype(F32)
    pb = pl.multiple_of(jnp.maximum(base - 16, 0), 16)
    nb = pl.multiple_of(jnp.minimum(base + rows, n_total - 16), 16)
    prev = src_ref[0, pl.ds(pb, 16), :].astype(F32)[15:16]
    nxt = src_ref[0, pl.ds(nb, 16), :].astype(F32)[0:1]
    prev = jnp.where(base > 0, prev, 0.0)
    nxt = jnp.where(base + rows < n_total, nxt, 0.0)
    ridx = lax.broadcasted_iota(I32, x.shape, 0)
    xp = jnp.where(ridx == 0, prev, pltpu.roll(x, 1, 0))
    xn = jnp.where(ridx == rows - 1, nxt, pltpu.roll(x, rows - 1, 0))
    y = xp * w[0:1] + x * w[1:2] + xn * w[2:3]
    return _silu(y)


def _l2norm_rows(t):
    return t * lax.rsqrt(jnp.sum(t * t, axis=-1, keepdims=True) + EPS)


def _gdn_kernel(q_ref, k_ref, v_ref, z_ref, gc_ref, gt_ref,
                qc_ref, kc_ref, vc_ref, gcc_ref, gtc_ref,
                cq_ref, ck_ref, cv_ref, gn_ref,
                y_ref,
                qs, ks, vs, osum, st, u_s, w_s, *, n_lat, n_ctx):
    hq = pl.program_id(1)
    rep = N_V_HEADS // N_QK_HEADS
    wq = cq_ref[...]
    wk = ck_ref[...]
    wv = cv_ref[...]

    def prep(src_q, src_k, src_v, n_total, row0):
        tile = _pick(n_total, 256, 16)

        def body(t, carry):
            base = pl.multiple_of(t * tile, tile)
            dst = pl.multiple_of(row0 + base, 16)
            qs[pl.ds(dst, tile), :] = _l2norm_rows(_conv_silu_tile(src_q, wq, base, tile, n_total)) * (HEAD_K ** -0.5)
            ks[pl.ds(dst, tile), :] = _l2norm_rows(_conv_silu_tile(src_k, wk, base, tile, n_total))
            vs[pl.ds(dst, tile), :] = _conv_silu_tile(src_v, wv, base, tile, n_total)
            return carry

        lax.fori_loop(0, n_total // tile, body, 0)

    prep(qc_ref, kc_ref, vc_ref, n_ctx, 0)
    prep(q_ref, k_ref, v_ref, n_lat, n_ctx)

    st[...] = jnp.zeros(st.shape, F32)
    osum[...] = jnp.zeros(osum.shape, F32)

    wide = (CHUNK, 2 * CHUNK)
    ri = lax.broadcasted_iota(I32, wide, 0)
    li = lax.broadcasted_iota(I32, wide, 1)
    ci = li % CHUNK
    left = li < CHUNK
    eye_left = jnp.where(left & (ri == ci), 1.0, 0.0)
    incl = (ri >= ci, ri <= ci)
    strict = (ri > ci, ri < ci)
    last = (CHUNK - 1, 0)
    lane = lax.broadcasted_iota(I32, (CHUNK, N_GATE_COLS), 1)
    zeros_sq = jnp.zeros(wide, BF16)
    zeros_uw = jnp.zeros((CHUNK, 2 * HEAD_V), BF16)

    def gate_col(gtile, col):
        return jnp.sum(jnp.where(lane == col, gtile, 0.0), axis=1, keepdims=True)

    def decay_of(gcol, grow2, dirn):
        return jnp.where(incl[dirn], jnp.exp(jnp.where(incl[dirn], gcol - grow2, 0.0)), 0.0)

    def mm(a, b):
        return jnp.dot(a, b, preferred_element_type=F32)

    grp = GDN_LOCAL_GROUP

    def chunk_of(seq, g, j, dirn):
        n_chunks = seq[3]
        return grp * g + j if dirn == 0 else n_chunks - 1 - grp * g - j

    def local_stages(seq, g, slot):
        gcol_ref, grow_ref, row0, _ = seq
        box = {}

        def prep():
            sbs, rhss, dests = [], [], []
            for j in range(grp):
                for dirn in range(2):
                    c = chunk_of(seq, g, j, dirn)
                    r = pl.multiple_of(row0 + c * CHUNK, CHUNK)
                    k = ks[pl.ds(r, CHUNK), :]
                    vv = vs[pl.ds(r, CHUNK), :]
                    k16 = k.astype(BF16)
                    kk2 = _dot_nt(k16, jnp.concatenate([k16, k16], axis=0))
                    gtile = gcol_ref[0, pl.ds(pl.multiple_of(c * CHUNK, CHUNK), CHUNK), :]
                    for head in range(rep):
                        col_g = (2 * dirn) * N_V_HEADS + hq * rep + head
                        gcol = gate_col(gtile, col_g)
                        bcol = gate_col(gtile, col_g + N_V_HEADS)
                        grow2 = grow_ref[0, c, 2 * dirn, pl.ds(head, 1), :]
                        a2 = bcol * kk2 * decay_of(gcol, grow2, dirn)
                        sbs.append(jnp.where(left, eye_left, jnp.where(strict[dirn], -a2, 0.0)))
                        v = vv[:, head * HEAD_V:(head + 1) * HEAD_V]
                        rhss.append(jnp.concatenate([v * bcol, k * (bcol * jnp.exp(gcol))], axis=1).astype(BF16))
                        dests.append((dirn * rep + head, j))
            box.update(sbs=sbs, rhss=rhss, dests=dests)

        def double():
            nxt = []
            for sb in box["sbs"]:
                sb16 = sb.astype(BF16)
                nxt.append(mm(sb16, jnp.concatenate([zeros_sq, sb16], axis=0)) + jnp.where(left, sb, 0.0))
            box["sbs"] = nxt

        def apply_s32():
            box["sb16s"] = [sb.astype(BF16) for sb in box["sbs"]]
            box["xs"] = [mm(sb16, jnp.concatenate([rhs, zeros_uw], axis=0))
                         for sb16, rhs in zip(box["sb16s"], box["rhss"])]

        def finish():
            for sb16, x, (chain, j) in zip(box["sb16s"], box["xs"], box["dests"]):
                uw = x + mm(sb16, jnp.concatenate([zeros_uw, x.astype(BF16)], axis=0))
                u_s[slot, chain, j * CHUNK:(j + 1) * CHUNK, :] = uw[:, :HEAD_V]
                w_s[slot, chain, j * CHUNK:(j + 1) * CHUNK, :] = uw[:, HEAD_V:].astype(BF16)

        return [prep] + [double] * 5 + [apply_s32, finish]

    def scan_stages(seq, g, j, slot, write_out):
        gcol_ref, grow_ref, row0, _ = seq
        box = {}

        def inputs():
            work = []
            for dirn in range(2):
                c = chunk_of(seq, g, j, dirn)
                r = pl.multiple_of(row0 + c * CHUNK, CHUNK)
                lr = pl.multiple_of(c * CHUNK, CHUNK)
                k = ks[pl.ds(r, CHUNK), :]
                gtile = gcol_ref[0, pl.ds(lr, CHUNK), :]
                if write_out:
                    q = qs[pl.ds(r, CHUNK), :]
                    k16 = k.astype(BF16)
                    qk2 = _dot_nt(q, jnp.concatenate([k16, k16], axis=0))
                for head in range(rep):
                    chain = dirn * rep + head
                    gcol = gate_col(gtile, (2 * dirn) * N_V_HEADS + hq * rep + head)
                    glast = gcol[last[dirn]:last[dirn] + 1, :]
                    kdec16 = (k * jnp.exp(glast - gcol)).astype(BF16)
                    lhs = w_s[slot, chain, j * CHUNK:(j + 1) * CHUNK, :]
                    p16 = None
                    if write_out:
                        grow2 = grow_ref[0, c, 2 * dirn, pl.ds(head, 1), :]
                        p16 = (qk2 * decay_of(gcol, grow2, dirn)).astype(BF16)
                        lhs = jnp.concatenate([lhs, (q * jnp.exp(gcol)).astype(BF16)], axis=0)
                    work.append((chain, lr, head, glast, kdec16, lhs, p16))
            box["work"] = work

        def state_matmul():
            box["s_olds"] = [st[wk[0]] for wk in box["work"]]
            box["tbs"] = [mm(wk[5], s_old.astype(BF16))
                          for wk, s_old in zip(box["work"], box["s_olds"])]

        def update():
            for (chain, lr, head, glast, kdec16, _, p16), s_old, tb in zip(box["work"], box["s_olds"], box["tbs"]):
                vn16 = (u_s[slot, chain, j * CHUNK:(j + 1) * CHUNK, :] - tb[:CHUNK]).astype(BF16)
                if write_out:
                    o = tb[CHUNK:] + mm(p16, jnp.concatenate([vn16, zeros_sq], axis=0))
                    osum[pl.ds(lr, CHUNK), head * HEAD_V:(head + 1) * HEAD_V] += o
                st[chain] = s_old * jnp.exp(glast) + _dot_tn(kdec16, vn16)

        return [inputs, state_matmul, update]

    def run_group(scan_of, local_of):
        local = local_stages(*local_of) if local_of else []
        steps = [scan_stages(scan_of[0], scan_of[1], j, scan_of[2], scan_of[3]) for j in range(grp)] if scan_of else []
        pending = list(local)
        for step in steps:
            step[0]()
            for stage in step[1:]:
                if pending:
                    pending.pop(0)()
                stage()
        for stage in pending:
            stage()

    ctx_seq = (gcc_ref, gtc_ref, 0, n_ctx // CHUNK)
    lat_seq = (gc_ref, gt_ref, n_ctx, n_lat // CHUNK)
    ng_ctx, ng_lat = n_ctx // CHUNK // grp, n_lat // CHUNK // grp
    run_group(None, (ctx_seq, 0, 0))
    for g in range(ng_ctx - 1):
        run_group((ctx_seq, g, g % 2, False), (ctx_seq, g + 1, (g + 1) % 2))
    run_group((ctx_seq, ng_ctx - 1, (ng_ctx - 1) % 2, False), (lat_seq, 0, ng_ctx % 2))

    def lat_body(g, carry):
        run_group((lat_seq, g, (ng_ctx + g) % 2, True), (lat_seq, g + 1, (ng_ctx + g + 1) % 2))
        return carry

    lax.fori_loop(0, ng_lat - 1, lat_body, 0)
    run_group((lat_seq, ng_lat - 1, (ng_ctx + ng_lat - 1) % 2, True), None)

    gn = gn_ref[...]
    tile = _pick(n_lat, 256, 16)

    def epi(t, carry):
        base = pl.multiple_of(t * tile, tile)
        o = osum[pl.ds(base, tile), :]
        z = z_ref[0, pl.ds(base, tile), :].astype(F32)
        for head in range(rep):
            sl = slice(head * HEAD_V, (head + 1) * HEAD_V)
            oh = o[:, sl]
            y = oh * lax.rsqrt(jnp.mean(oh * oh, axis=-1, keepdims=True) + EPS) * gn * _silu(z[:, sl])
            y_ref[0, pl.ds(base, tile), sl] = y.astype(y_ref.dtype)
        return carry

    lax.fori_loop(0, n_lat // tile, epi, 0)


def _gdn(p_qkv, p_z, gcol, grow, pc_qkv, gcol_c, grow_c, conv_qkv, gdn_norm):
    b, n_lat, _ = p_qkv.shape
    n_ctx = pc_qkv.shape[1]
    rep = N_V_HEADS // N_QK_HEADS
    vw = rep * HEAD_V
    kq0 = QK_DIM // HEAD_K
    v0 = 2 * QK_DIM // vw
    nc_lat, nc_ctx = n_lat // CHUNK, n_ctx // CHUNK
    assert nc_lat % GDN_LOCAL_GROUP == 0 and nc_ctx % GDN_LOCAL_GROUP == 0
    assert HEAD_K == HEAD_V == N_GATE_COLS == 2 * CHUNK
    grow5 = grow.reshape(b, nc_lat, 4, N_QK_HEADS, rep, 2 * CHUNK)
    grow5_c = grow_c.reshape(b, nc_ctx, 4, N_QK_HEADS, rep, 2 * CHUNK)

    def seq_specs(n):
        return [pl.BlockSpec((1, n, HEAD_K), lambda i, h: (i, 0, h)),
                pl.BlockSpec((1, n, HEAD_K), lambda i, h: (i, 0, kq0 + h)),
                pl.BlockSpec((1, n, vw), lambda i, h: (i, 0, v0 + h))]

    def gate_specs(n, nc):
        return [pl.BlockSpec((1, n, N_GATE_COLS), lambda i, h: (i, 0, 0)),
                pl.BlockSpec((1, nc, 4, None, rep, 2 * CHUNK), lambda i, h: (i, 0, 0, h, 0, 0))]

    in_specs = (seq_specs(n_lat)
                + [pl.BlockSpec((1, n_lat, vw), lambda i, h: (i, 0, h))]
                + gate_specs(n_lat, nc_lat)
                + seq_specs(n_ctx)
                + gate_specs(n_ctx, nc_ctx)
                + [pl.BlockSpec((3, HEAD_K), lambda i, h: (0, h)),
                   pl.BlockSpec((3, HEAD_K), lambda i, h: (0, kq0 + h)),
                   pl.BlockSpec((3, vw), lambda i, h: (0, v0 + h)),
                   pl.BlockSpec((1, HEAD_V), lambda i, h: (0, 0))])
    n_all = n_ctx + n_lat
    return pl.pallas_call(
        functools.partial(_gdn_kernel, n_lat=n_lat, n_ctx=n_ctx),
        out_shape=jax.ShapeDtypeStruct((b, n_lat, V_DIM), BF16),
        grid=(b, N_QK_HEADS),
        in_specs=in_specs,
        out_specs=pl.BlockSpec((1, n_lat, vw), lambda i, h: (i, 0, h)),
        scratch_shapes=[pltpu.VMEM((n_all, HEAD_K), F32),
                        pltpu.VMEM((n_all, HEAD_K), F32),
                        pltpu.VMEM((n_all, vw), F32),
                        pltpu.VMEM((n_lat, vw), F32),
                        pltpu.VMEM((2 * rep, HEAD_K, HEAD_V), F32),
                        pltpu.VMEM((2, 2 * rep, GDN_LOCAL_GROUP * CHUNK, HEAD_V), F32),
                        pltpu.VMEM((2, 2 * rep, GDN_LOCAL_GROUP * CHUNK, HEAD_K), BF16)],
        compiler_params=_params(("parallel", "parallel")),
        name="gdn_bidir",
    )(p_qkv, p_qkv, p_qkv, p_z, gcol, grow5,
      pc_qkv, pc_qkv, pc_qkv, gcol_c, grow5_c,
      conv_qkv, conv_qkv, conv_qkv, gdn_norm)


def _conva_kernel(gb_ref, gcx_ref, xin_ref, w_ref, o_ref):
    u = gcx_ref[...].astype(F32) * xin_ref[...].astype(F32)
    rows = u.shape[0]
    pos = lax.broadcasted_iota(I32, u.shape, 0) % GRID_W
    up = jnp.where(pos == 0, 0.0, pltpu.roll(u, 1, 0))
    un = jnp.where(pos == GRID_W - 1, 0.0, pltpu.roll(u, rows - 1, 0))
    w = w_ref[...]
    conv = up * w[0:1] + u * w[1:2] + un * w[2:3]
    o_ref[...] = (gb_ref[...].astype(F32) * conv).astype(o_ref.dtype)


def _conva(p_conv, conv_w):
    t, c3 = p_conv.shape
    dc = c3 // 3
    tr = _pick(t, 512, GRID_W)
    tc = _pick(dc, 512, V7X_LANES)
    nj = dc // tc
    return pl.pallas_call(
        _conva_kernel,
        out_shape=jax.ShapeDtypeStruct((t, dc), BF16),
        grid=(t // tr, nj),
        in_specs=[pl.BlockSpec((tr, tc), lambda i, j: (i, j)),
                  pl.BlockSpec((tr, tc), lambda i, j: (i, nj + j)),
                  pl.BlockSpec((tr, tc), lambda i, j: (i, 2 * nj + j)),
                  pl.BlockSpec((3, tc), lambda i, j: (0, j))],
        out_specs=pl.BlockSpec((tr, tc), lambda i, j: (i, j)),
        compiler_params=_params(("parallel", "parallel")),
        name="mixer_a_conv",
    )(p_conv, p_conv, p_conv, conv_w)


def _mixout_kernel(ya_ref, yb_ref, ga_ref, gb_ref, wo_ref, x_ref, mod_ref, o_ref, *, gate_idx):
    mix = (_sigmoid(ga_ref[0].astype(F32)) * ya_ref[0].astype(F32)
           + _sigmoid(gb_ref[0].astype(F32)) * yb_ref[0].astype(F32))
    out = _dot(mix, wo_ref[...])
    m = mod_ref[0]
    o_ref[0] = x_ref[0] + m[gate_idx:gate_idx + 1] * out


def _mixout(ya, yb, p_gate, w_o, x, mod3):
    b, n, d = x.shape
    tm = _pick(n, 512, 16)
    return pl.pallas_call(
        functools.partial(_mixout_kernel, gate_idx=2),
        out_shape=jax.ShapeDtypeStruct((b, n, d), F32),
        grid=(b, n // tm),
        in_specs=[pl.BlockSpec((1, tm, d), lambda i, j: (i, j, 0)),
                  pl.BlockSpec((1, tm, d), lambda i, j: (i, j, 0)),
                  pl.BlockSpec((1, tm, d), lambda i, j: (i, j, 0)),
                  pl.BlockSpec((1, tm, d), lambda i, j: (i, j, 1)),
                  pl.BlockSpec((d, d), lambda i, j: (0, 0)),
                  pl.BlockSpec((1, tm, d), lambda i, j: (i, j, 0)),
                  pl.BlockSpec((1, 6, d), lambda i, j: (i, 0, 0))],
        out_specs=pl.BlockSpec((1, tm, d), lambda i, j: (i, j, 0)),
        compiler_params=_params(("parallel", "parallel")),
        name="merge_outproj",
    )(ya, yb, p_gate, p_gate, w_o, x, mod3)


def _router_kernel(x_ref, w_ref, mod_ref, wr_ref, bias_ref, h_ref, idx_ref, wgt_ref, *, shift_idx, scale_idx):
    x = x_ref[0]
    y = x * lax.rsqrt(jnp.mean(x * x, axis=-1, keepdims=True) + EPS) * w_ref[...]
    m = mod_ref[0]
    h = y * (1.0 + m[scale_idx:scale_idx + 1]) + m[shift_idx:shift_idx + 1]
    h_ref[0] = h
    tm = h.shape[0]
    h_hi = h.astype(BF16)
    h_lo = (h - h_hi.astype(F32)).astype(BF16)
    wr = wr_ref[...]
    w_hi = wr.astype(BF16)
    w_lo = (wr - w_hi.astype(F32)).astype(BF16)
    logits = _dot_nt(w_hi, h_hi) + _dot_nt(w_hi, h_lo) + _dot_nt(w_lo, h_hi)
    scores = _sigmoid(logits)
    biased = scores + bias_ref[...]
    per_group = N_EXPERTS // N_GROUPS
    neg = -jnp.inf
    i32g = lax.broadcasted_iota(I32, (per_group, tm), 0).astype(F32)
    groups, gscore = [], []
    for g in range(N_GROUPS):
        sg = biased[g * per_group:(g + 1) * per_group, :]
        m1 = jnp.max(sg, axis=0, keepdims=True)
        a1 = jnp.min(jnp.where(sg == m1, i32g, float(per_group)), axis=0, keepdims=True)
        m2 = jnp.max(jnp.where(i32g == a1, neg, sg), axis=0, keepdims=True)
        groups.append(sg)
        gscore.append(m1 + m2)
    parts = []
    for g in range(N_GROUPS):
        ahead = jnp.zeros((1, tm), F32)
        for o in range(N_GROUPS):
            if o == g:
                continue
            beats = (gscore[o] >= gscore[g]) if o < g else (gscore[o] > gscore[g])
            ahead = ahead + jnp.where(beats, 1.0, 0.0)
        parts.append(jnp.where(ahead < float(TOPK_GROUPS), groups[g], neg))
    masked = jnp.concatenate(parts, axis=0)
    ei = lax.broadcasted_iota(I32, masked.shape, 0).astype(F32)
    idxs, wgts = [], []
    for _ in range(TOP_K):
        em = jnp.max(masked, axis=0, keepdims=True)
        ea = jnp.min(jnp.where(masked == em, ei, float(N_EXPERTS)), axis=0, keepdims=True)
        hit = ei == ea
        idxs.append(ea)
        wgts.append(jnp.sum(jnp.where(hit, scores, 0.0), axis=0, keepdims=True))
        masked = jnp.where(hit, neg, masked)
    idx = jnp.concatenate(idxs, axis=0)
    wgt = jnp.concatenate(wgts, axis=0)
    wgt = wgt / jnp.sum(wgt, axis=0, keepdims=True) * ROUTED_SCALE
    idx_ref[0] = idx.astype(I32)
    wgt_ref[0] = wgt


def _router(x1, norm_w, mod3, wr_t, bias_col):
    b, n, d = x1.shape
    tm = _pick(n, 256, V7X_LANES)
    nt = n // tm
    return pl.pallas_call(
        functools.partial(_router_kernel, shift_idx=3, scale_idx=4),
        out_shape=(jax.ShapeDtypeStruct((b, n, d), F32),
                   jax.ShapeDtypeStruct((b, TOP_K, n), I32),
                   jax.ShapeDtypeStruct((b, TOP_K, n), F32)),
        grid=(b, nt),
        in_specs=[pl.BlockSpec((1, tm, d), lambda i, j: (i, j, 0)),
                  pl.BlockSpec((1, d), lambda i, j: (0, 0)),
                  pl.BlockSpec((1, 6, d), lambda i, j: (i, 0, 0)),
                  pl.BlockSpec((N_EXPERTS, d), lambda i, j: (0, 0)),
                  pl.BlockSpec((N_EXPERTS, 1), lambda i, j: (0, 0))],
        out_specs=(pl.BlockSpec((1, tm, d), lambda i, j: (i, j, 0)),
                   pl.BlockSpec((1, TOP_K, tm), lambda i, j: (i, 0, j)),
                   pl.BlockSpec((1, TOP_K, tm), lambda i, j: (i, 0, j))),
        compiler_params=_params(("parallel", "parallel")),
        name="ffn_norm_router",
    )(x1, norm_w, mod3, wr_t, bias_col)


def _spread(total, weights):
    acc, out, lo = 0.0, [], 0
    for w in weights:
        acc += w
        hi = round(total * acc / sum(weights))
        out.append(range(lo, hi))
        lo = hi
    return out


def _experts_kernel(be_ref, nu_ref, tok_ref, slot_ref, h_hbm, wg_ref, wu_ref, wd_ref, y_hbm,
                    xbuf, ybuf, gsem, ssem):
    j = pl.program_id(0)
    n_used = nu_ref[0]
    bm, d = xbuf.shape[1], xbuf.shape[2]
    de = wg_ref.shape[2]
    mxu = V7X_MXU_COLS

    def wait_step_dmas():
        pltpu.make_async_copy(h_hbm.at[pl.ds(0, bm)], xbuf.at[0], gsem).wait()
        pltpu.make_async_copy(ybuf.at[0], y_hbm.at[pl.ds(0, bm)], ssem).wait()

    @pl.when(j == 0)
    def _():
        xbuf[...] = jnp.zeros(xbuf.shape, F32)
        ybuf[...] = jnp.zeros(ybuf.shape, F32)

    @pl.when((j >= 1) & (j <= n_used + 1))
    def _():
        wait_step_dmas()

    def step(nxt):
        cur = 1 - nxt

        def move_rows(rows):
            for r in rows:
                pltpu.make_async_copy(h_hbm.at[pl.ds(tok_ref[0, 0, r], 1)],
                                      xbuf.at[nxt, pl.ds(r, 1)], gsem).start()
                pltpu.make_async_copy(ybuf.at[nxt, pl.ds(r, 1)],
                                      y_hbm.at[pl.ds(slot_ref[0, 0, r], 1)], ssem).start()

        n_up, n_down = de // mxu, d // mxu
        parts = _spread(bm, [float(d)] * (2 * n_up) + [float(de)] * n_down)
        x = xbuf[cur].astype(BF16)
        hid = []
        for s in range(n_up):
            cols = slice(s * mxu, (s + 1) * mxu)
            g = jnp.dot(x, wg_ref[0, :, cols].astype(BF16), preferred_element_type=F32)
            move_rows(parts[2 * s])
            u = jnp.dot(x, wu_ref[0, :, cols].astype(BF16), preferred_element_type=F32)
            move_rows(parts[2 * s + 1])
            hid.append((_silu(g) * u).astype(BF16))
        hid = jnp.concatenate(hid, axis=1)
        for s in range(n_down):
            cols = slice(s * mxu, (s + 1) * mxu)
            ybuf[cur, :, cols] = jnp.dot(hid, wd_ref[0, :, cols].astype(BF16), preferred_element_type=F32)
            move_rows(parts[2 * n_up + s])

    for parity in range(2):
        @pl.when((j <= n_used + 1) & (j % 2 == parity))
        def _(parity=parity):
            step(parity)

    @pl.when(j == n_used + 1)
    def _():
        wait_step_dmas()


def _experts(h2, block_expert, n_used, tok_buf, slot_buf, w_gate, w_up, w_down):
    t, d = h2.shape
    de = w_gate.shape[2]
    n_blocks, _, bm = tok_buf.shape
    assert de % V7X_MXU_COLS == 0 and d % V7X_MXU_COLS == 0
    last = n_blocks - 1

    def weights_of(j, be, nu):
        return (be[jnp.maximum(j - 1, 0)], 0, 0)

    grid_spec = pltpu.PrefetchScalarGridSpec(
        num_scalar_prefetch=2,
        grid=(n_blocks + 1,),
        in_specs=[
            pl.BlockSpec((1, 1, bm), lambda j, be, nu: (jnp.minimum(j, last), 0, 0), memory_space=pltpu.SMEM),
            pl.BlockSpec((1, 1, bm), lambda j, be, nu: (jnp.where(j >= 2, jnp.minimum(j - 2, last), last), 0, 0),
                         memory_space=pltpu.SMEM),
            pl.BlockSpec(memory_space=pl.ANY),
            pl.BlockSpec((1, d, de), weights_of),
            pl.BlockSpec((1, d, de), weights_of),
            pl.BlockSpec((1, de, d), weights_of),
        ],
        out_specs=pl.BlockSpec(memory_space=pl.ANY),
        scratch_shapes=[pltpu.VMEM((2, bm, d), F32),
                        pltpu.VMEM((2, bm, d), F32),
                        pltpu.SemaphoreType.DMA,
                        pltpu.SemaphoreType.DMA],
    )
    return pl.pallas_call(
        _experts_kernel,
        out_shape=jax.ShapeDtypeStruct((TOP_K * t + bm, d), F32),
        grid_spec=grid_spec,
        compiler_params=_params(("arbitrary",)),
        name="routed_experts",
    )(block_expert, n_used, tok_buf, slot_buf, h2, w_gate, w_up, w_down)


def _shared_kernel(h_ref, wg_ref, wu_ref, wd_ref, o_ref):
    x = h_ref[...].astype(BF16)
    hid = _silu(jnp.dot(x, wg_ref[...], preferred_element_type=F32)) * jnp.dot(
        x, wu_ref[...], preferred_element_type=F32)
    o_ref[...] = jnp.dot(hid.astype(BF16), wd_ref[...], preferred_element_type=F32).astype(o_ref.dtype)


def _shared(h2, wg, wu, wd):
    t, d = h2.shape
    ds = wg.shape[1]
    tm = _pick(t, 512, 16)
    return pl.pallas_call(
        _shared_kernel,
        out_shape=jax.ShapeDtypeStruct((t, d), F32),
        grid=(t // tm,),
        in_specs=[pl.BlockSpec((tm, d), lambda i: (i, 0)),
                  pl.BlockSpec((d, ds), lambda i: (0, 0)),
                  pl.BlockSpec((d, ds), lambda i: (0, 0)),
                  pl.BlockSpec((ds, d), lambda i: (0, 0))],
        out_specs=pl.BlockSpec((tm, d), lambda i: (i, 0)),
        compiler_params=_params(("parallel",)),
        name="shared_expert",
    )(h2, wg, wu, wd)


def _combine_kernel(x_ref, wt_ref, sh_ref, mod_ref, fw_ref, *rest, gate_idx):
    yk_refs, o_ref = rest[:TOP_K], rest[TOP_K]
    wt = wt_ref[0]
    acc = sh_ref[0]
    for kk in range(TOP_K):
        acc = acc + wt[:, kk:kk + 1] * yk_refs[kk][...]
    m = mod_ref[0]
    x = x_ref[0] + m[gate_idx:gate_idx + 1] * acc
    o_ref[0] = x * lax.rsqrt(jnp.mean(x * x, axis=-1, keepdims=True) + EPS) * fw_ref[...]


def _combine(x1, yk, wtok, shared, mod3, final_w):
    b, n, d = x1.shape
    tm = _pick(n, 128, 16)
    nt = n // tm

    def yk_spec(kk):
        return pl.BlockSpec((tm, d), lambda i, j: ((kk * b + i) * nt + j, 0))

    return pl.pallas_call(
        functools.partial(_combine_kernel, gate_idx=5),
        out_shape=jax.ShapeDtypeStruct((b, n, d), F32),
        grid=(b, nt),
        in_specs=[pl.BlockSpec((1, tm, d), lambda i, j: (i, j, 0)),
                  pl.BlockSpec((1, tm, TOP_K), lambda i, j: (i, j, 0)),
                  pl.BlockSpec((1, tm, d), lambda i, j: (i, j, 0)),
                  pl.BlockSpec((1, 6, d), lambda i, j: (i, 0, 0)),
                  pl.BlockSpec((1, d), lambda i, j: (0, 0))] + [yk_spec(kk) for kk in range(TOP_K)],
        out_specs=pl.BlockSpec((1, tm, d), lambda i, j: (i, j, 0)),
        compiler_params=_params(("parallel", "parallel")),
        name="combine_final_norm",
    )(x1, wtok, shared, mod3, final_w, *([yk] * TOP_K))


def _routing_tables(idx_kt, n_tokens):
    bm = EXPERT_BLOCK_ROWS
    n_assign = TOP_K * n_tokens
    assert n_assign % bm == 0
    n_blocks = n_assign // bm + N_EXPERTS
    e_flat = idx_kt.reshape(n_assign)
    experts = jnp.arange(N_EXPERTS, dtype=I32)
    counts = jnp.sum((e_flat[None, :] == experts[:, None]).astype(I32), axis=1)
    n_pad = (-counts) % bm
    big = 2 * N_EXPERTS
    fill = jnp.arange(bm, dtype=I32)[None, :] < n_pad[:, None]
    fill_keys = jnp.where(fill, 2 * experts[:, None] + 1, big).reshape(N_EXPERTS * bm)
    keys = jnp.concatenate([2 * e_flat, fill_keys])
    vals = jnp.concatenate([jnp.arange(n_assign, dtype=I32), jnp.full((N_EXPERTS * bm,), -1, I32)])
    keys, vals = lax.sort((keys, vals), num_keys=1)
    valid = vals >= 0
    tok = jnp.where(valid, vals % n_tokens, 0)
    slot = jnp.where(valid, vals, n_assign + jnp.arange(n_blocks * bm, dtype=I32) % bm)
    n_used = (jnp.sum(counts + n_pad) // bm).astype(I32)
    blk = jnp.minimum(jnp.arange(n_blocks, dtype=I32), n_used - 1)
    block_expert = keys.reshape(n_blocks, bm)[:, 0][blk] // 2
    return block_expert, n_used.reshape(1), tok.reshape(n_blocks, 1, bm), slot.reshape(n_blocks, 1, bm)


def kernel(x, c, ctx, c_ctx, w_ada, b_ada, norm_mix, norm_ffn, w_in, conv_a, conv_qkv, a_log, dt_bias,
           gdn_norm, w_proj_a, w_proj_b, w_out, w_router, router_bias, w_gate, w_up, w_down,
           ws_gate, ws_up, ws_down, final_norm):
    assert w_ada.shape[0] == 1, "single (final) layer only"
    b, seq, d = x.shape
    n_ctx = ctx.shape[1]
    t = b * seq
    dc = conv_a.shape[2]
    assert seq % GRID_W == 0 and seq % CHUNK == 0 and n_ctx % CHUNK == 0
    z_off = QKV_DIM + N_GATE_COLS
    conv_off = z_off + V_DIM
    gate_off = conv_off + 3 * dc
    assert w_in.shape[2] == gate_off + 2 * d

    rows = -(-(b + 1) // 8) * 8
    cpad = jnp.zeros((rows, d), F32).at[:b].set(c).at[b].set(c_ctx)
    mod3 = _ada(cpad, w_ada[0], b_ada).reshape(rows, 6, d)

    h = _norm_mod(x, norm_mix, mod3, lambda i: i, 0, 1, BF16).reshape(t, d)
    hc = _norm_mod(ctx, norm_mix, mod3, lambda i: b, 0, 1, BF16).reshape(b * n_ctx, d)
    w_in0 = w_in[0]
    w_qkv = w_in0[:, :QKV_DIM].astype(BF16)
    w_ab = w_in0[:, QKV_DIM:z_off].astype(BF16)
    p_qkv = _mm(h, w_qkv, BF16, "inproj_qkv").reshape(b, seq, QKV_DIM)
    p_ab = _mm(h, w_ab, F32, "inproj_ab").reshape(b, seq, N_GATE_COLS)
    p_z = _mm(h, w_in0[:, z_off:conv_off].astype(BF16), BF16, "inproj_z").reshape(b, seq, V_DIM)
    p_conv = _mm(h, w_in0[:, conv_off:gate_off].astype(BF16), BF16, "inproj_conv")
    p_gate = _mm(h, w_in0[:, gate_off:].astype(BF16), BF16, "inproj_gate").reshape(b, seq, 2 * d)
    pc_qkv = _mm(hc, w_qkv, BF16, "inproj_qkv_ctx").reshape(b, n_ctx, QKV_DIM)
    pc_ab = _mm(hc, w_ab, F32, "inproj_ab_ctx").reshape(b, n_ctx, N_GATE_COLS)

    zero = jnp.zeros((N_V_HEADS,), F32)
    prm = jnp.stack([jnp.concatenate([a_log[0, 0], zero, a_log[0, 1], zero]),
                     jnp.concatenate([dt_bias[0, 0], zero, dt_bias[0, 1], zero])])
    gcol, grow = _gates(p_ab, prm)
    gcol_c, grow_c = _gates(pc_ab, prm)
    yb_in = _gdn(p_qkv, p_z, gcol, grow, pc_qkv, gcol_c, grow_c, conv_qkv[0], gdn_norm)

    ya_in = _conva(p_conv, conv_a[0])
    ya = _mm(ya_in, w_proj_a[0].astype(BF16), BF16, "proj_a").reshape(b, seq, d)
    yb = _mm(yb_in.reshape(t, V_DIM), w_proj_b[0].astype(BF16), BF16, "proj_b").reshape(b, seq, d)
    x1 = _mixout(ya, yb, p_gate, w_out[0].astype(BF16), x, mod3)

    h2, idx_bkt, wgt_bkt = _router(x1, norm_ffn, mod3, w_router[0].T, router_bias[0].reshape(N_EXPERTS, 1))
    h2 = h2.reshape(t, d)
    idx_kt = jnp.transpose(idx_bkt, (1, 0, 2)).reshape(TOP_K, t)
    wtok = jnp.transpose(wgt_bkt, (0, 2, 1))
    block_expert, n_used, tok_buf, slot_buf = _routing_tables(idx_kt, t)
    yk = _experts(h2, block_expert, n_used, tok_buf, slot_buf, w_gate[0], w_up[0], w_down[0])
    shared = _shared(h2, ws_gate[0].astype(BF16), ws_up[0].astype(BF16), ws_down[0].astype(BF16)).reshape(b, seq, d)
    return _combine(x1, yk, wtok, shared, mod3, final_norm.reshape(1, d))
```

```python
import functools

import jax
import jax.numpy as jnp
from jax import lax
from jax.experimental import pallas as pl
from jax.experimental.pallas import tpu as pltpu

F32 = jnp.float32
BF16 = jnp.bfloat16
I32 = jnp.int32

EPS = 1e-6
GRID_W = 64
N_QK_HEADS = 16
N_V_HEADS = 32
HEAD_K = 128
HEAD_V = 128
CHUNK = 64
N_EXPERTS = 256
N_GROUPS = 8
TOPK_GROUPS = 4
TOP_K = 8
ROUTED_SCALE = 2.5

QK_DIM = N_QK_HEADS * HEAD_K
V_DIM = N_V_HEADS * HEAD_V
QKV_DIM = 2 * QK_DIM + V_DIM
N_GATE_COLS = 4 * N_V_HEADS

V7X_LANES = 128
V7X_VMEM_LIMIT_BYTES = 56 * 1024 * 1024
V7X_MXU_COLS = 256
EXPERT_BLOCK_ROWS = 256
EXPERT_SLAB_ROWS = 17
GDN_LOCAL_GROUP = 4


def _pick(n, target, mult):
    if n <= target:
        return n
    t = (target // mult) * mult
    while t >= mult:
        if n % t == 0:
            return t
        t -= mult
    raise ValueError(f"no tile for {n} (target {target}, multiple {mult})")


def _params(sem, vmem=V7X_VMEM_LIMIT_BYTES):
    return pltpu.CompilerParams(dimension_semantics=sem, vmem_limit_bytes=vmem)


def _sigmoid(x):
    return 1.0 / (1.0 + jnp.exp(-x))


def _silu(x):
    return x * _sigmoid(x)


def _dot(a, b):
    return jnp.dot(a.astype(BF16), b.astype(BF16), preferred_element_type=F32)


def _dot_nt(a, b):
    return lax.dot_general(a.astype(BF16), b.astype(BF16), (((1,), (1,)), ((), ())),
                           preferred_element_type=F32)


def _dot_tn(a, b):
    return lax.dot_general(a.astype(BF16), b.astype(BF16), (((0,), (0,)), ((), ())),
                           preferred_element_type=F32)


def _split3(x):
    hi = x.astype(BF16)
    r = x - hi.astype(F32)
    mid = r.astype(BF16)
    lo = (r - mid.astype(F32)).astype(BF16)
    return hi, mid, lo


def _ada_kernel(c_ref, w_ref, b_ref, o_ref):
    a = _silu(c_ref[...])
    o_ref[...] = _dot(a, w_ref[...]) + b_ref[...]


def _ada(cpad, w, b):
    rows, d = cpad.shape
    n = w.shape[1]
    tn = _pick(n, 1024, V7X_LANES)
    return pl.pallas_call(
        _ada_kernel,
        out_shape=jax.ShapeDtypeStruct((rows, n), F32),
        grid=(n // tn,),
        in_specs=[pl.BlockSpec((rows, d), lambda j: (0, 0)),
                  pl.BlockSpec((d, tn), lambda j: (0, j)),
                  pl.BlockSpec((1, tn), lambda j: (0, j))],
        out_specs=pl.BlockSpec((rows, tn), lambda j: (0, j)),
        compiler_params=_params(("parallel",)),
        name="ada_mod",
    )(cpad, w, b)


def _norm_mod_kernel(x_ref, w_ref, mod_ref, o_ref, *, shift_idx, scale_idx):
    x = x_ref[0]
    y = x * lax.rsqrt(jnp.mean(x * x, axis=-1, keepdims=True) + EPS) * w_ref[...]
    m = mod_ref[0]
    o_ref[0] = (y * (1.0 + m[scale_idx:scale_idx + 1]) + m[shift_idx:shift_idx + 1]).astype(o_ref.dtype)


def _norm_mod(x, w, mod3, row_of_batch, shift_idx, scale_idx, out_dtype):
    b, n, d = x.shape
    ts = _pick(n, 512, 16)
    return pl.pallas_call(
        functools.partial(_norm_mod_kernel, shift_idx=shift_idx, scale_idx=scale_idx),
        out_shape=jax.ShapeDtypeStruct((b, n, d), out_dtype),
        grid=(b, n // ts),
        in_specs=[pl.BlockSpec((1, ts, d), lambda i, j: (i, j, 0)),
                  pl.BlockSpec((1, d), lambda i, j: (0, 0)),
                  pl.BlockSpec((1, 6, d), lambda i, j: (row_of_batch(i), 0, 0))],
        out_specs=pl.BlockSpec((1, ts, d), lambda i, j: (i, j, 0)),
        compiler_params=_params(("parallel", "parallel")),
        name="norm_mod",
    )(x, w, mod3)


def _mm_kernel(a_ref, w_ref, o_ref):
    o_ref[...] = jnp.dot(a_ref[...], w_ref[...], preferred_element_type=F32).astype(o_ref.dtype)


def _mm(a, w, out_dtype, name):
    m, k = a.shape
    n = w.shape[1]
    tm = _pick(m, 1024, 16)
    tn = _pick(n, 1024, V7X_LANES)
    return pl.pallas_call(
        _mm_kernel,
        out_shape=jax.ShapeDtypeStruct((m, n), out_dtype),
        grid=(n // tn, m // tm),
        in_specs=[pl.BlockSpec((tm, k), lambda j, i: (i, 0)),
                  pl.BlockSpec((k, tn), lambda j, i: (0, j))],
        out_specs=pl.BlockSpec((tm, tn), lambda j, i: (i, j)),
        compiler_params=_params(("parallel", "parallel")),
        name=name,
    )(a, w)


def _gates_kernel(ab_ref, prm_ref, col_ref, row_ref, *, rows):
    sub = 2 * CHUNK
    n_sub = rows // sub
    prm = prm_ref[...]
    alog = prm[0:1]
    dtb = prm[1:2]
    lane = lax.broadcasted_iota(I32, (sub, N_GATE_COLS), 1)
    is_g = (lane // N_V_HEADS) % 2 == 0
    is_bwd = lane >= 2 * N_V_HEADS
    r = lax.broadcasted_iota(I32, (sub, sub), 0)
    c = lax.broadcasted_iota(I32, (sub, sub), 1)
    same = (r // CHUNK) == (c // CHUNK)
    cum_f = jnp.where(same & (c <= r), 1.0, 0.0).astype(BF16)
    cum_b = jnp.where(same & (c >= r), 1.0, 0.0).astype(BF16)
    for s in range(n_sub):
        x = ab_ref[0, s * sub:(s + 1) * sub, :]
        xs = x + dtb
        softplus = jnp.maximum(xs, 0.0) + jnp.log(1.0 + jnp.exp(-jnp.abs(xs)))
        g = -jnp.exp(alog) * softplus
        beta = _sigmoid(x)
        hi, mid, lo = _split3(g)
        pf = (jnp.dot(cum_f, hi, preferred_element_type=F32) + jnp.dot(cum_f, mid, preferred_element_type=F32)
              + jnp.dot(cum_f, lo, preferred_element_type=F32))
        pb = (jnp.dot(cum_b, hi, preferred_element_type=F32) + jnp.dot(cum_b, mid, preferred_element_type=F32)
              + jnp.dot(cum_b, lo, preferred_element_type=F32))
        out = jnp.where(is_g, jnp.where(is_bwd, pb, pf), beta)
        col_ref[0, s * sub:(s + 1) * sub, :] = out
        for h in range(2):
            oc = out[h * CHUNK:(h + 1) * CHUNK]
            row_ref[0, 2 * s + h] = jnp.concatenate([oc, oc], axis=0).T


def _gates(ab, prm):
    b, n, g = ab.shape
    rows = _pick(n, 512, 2 * CHUNK)
    return pl.pallas_call(
        functools.partial(_gates_kernel, rows=rows),
        out_shape=(jax.ShapeDtypeStruct((b, n, g), F32),
                   jax.ShapeDtypeStruct((b, n // CHUNK, g, 2 * CHUNK), F32)),
        grid=(b, n // rows),
        in_specs=[pl.BlockSpec((1, rows, g), lambda i, j: (i, j, 0)),
                  pl.BlockSpec((2, g), lambda i, j: (0, 0))],
        out_specs=(pl.BlockSpec((1, rows, g), lambda i, j: (i, j, 0)),
                   pl.BlockSpec((1, rows // CHUNK, g, 2 * CHUNK), lambda i, j: (i, j, 0, 0))),
        compiler_params=_params(("parallel", "parallel")),
        name="gate_prep",
    )(ab, prm)


def _conv_silu_tile(src_ref, w, base, rows, n_total):
    x = src_ref[0, pl.ds(base, rows), :].astype(F32)
    pb = pl.multiple_of(jnp.maximum(base - 16, 0), 16)
    nb = pl.multiple_of(jnp.minimum(base + rows, n_total - 16), 16)
    prev = src_ref[0, pl.ds(pb, 16), :].astype(F32)[15:16]
    nxt = src_ref[0, pl.ds(nb, 16), :].astype(F32)[0:1]
    prev = jnp.where(base > 0, prev, 0.0)
    nxt = jnp.where(base + rows < n_total, nxt, 0.0)
    ridx = lax.broadcasted_iota(I32, x.shape, 0)
    xp = jnp.where(ridx == 0, prev, pltpu.roll(x, 1, 0))
    xn = jnp.where(ridx == rows - 1, nxt, pltpu.roll(x, rows - 1, 0))
    y = xp * w[0:1] + x * w[1:2] + xn * w[2:3]
    return _silu(y)


def _l2norm_rows(t):
    return t * lax.rsqrt(jnp.sum(t * t, axis=-1, keepdims=True) + EPS)


def _gdn_kernel(q_ref, k_ref, v_ref, z_ref, gc_ref, gt_ref,
                qc_ref, kc_ref, vc_ref, gcc_ref, gtc_ref,
                cq_ref, ck_ref, cv_ref, gn_ref,
                y_ref,
                qs, ks, vs, osum, st, u_s, w_s, *, n_lat, n_ctx):
    hq = pl.program_id(1)
    rep = N_V_HEADS // N_QK_HEADS
    wq = cq_ref[...]
    wk = ck_ref[...]
    wv = cv_ref[...]

    def prep(src_q, src_k, src_v, n_total, row0):
        tile = _pick(n_total, 256, 16)

        def body(t, carry):
            base = pl.multiple_of(t * tile, tile)
            dst = pl.multiple_of(row0 + base, 16)
            qs[pl.ds(dst, tile), :] = _l2norm_rows(_conv_silu_tile(src_q, wq, base, tile, n_total)) * (HEAD_K ** -0.5)
            ks[pl.ds(dst, tile), :] = _l2norm_rows(_conv_silu_tile(src_k, wk, base, tile, n_total))
            vs[pl.ds(dst, tile), :] = _conv_silu_tile(src_v, wv, base, tile, n_total)
            return carry

        lax.fori_loop(0, n_total // tile, body, 0)

    prep(qc_ref, kc_ref, vc_ref, n_ctx, 0)
    prep(q_ref, k_ref, v_ref, n_lat, n_ctx)

    st[...] = jnp.zeros(st.shape, F32)
    osum[...] = jnp.zeros(osum.shape, F32)

    wide = (CHUNK, 2 * CHUNK)
    ri = lax.broadcasted_iota(I32, wide, 0)
    li = lax.broadcasted_iota(I32, wide, 1)
    ci = li % CHUNK
    left = li < CHUNK
    eye_left = jnp.where(left & (ri == ci), 1.0, 0.0)
    incl = (ri >= ci, ri <= ci)
    strict = (ri > ci, ri < ci)
    last = (CHUNK - 1, 0)
    lane = lax.broadcasted_iota(I32, (CHUNK, N_GATE_COLS), 1)
    zeros_sq = jnp.zeros(wide, BF16)
    zeros_uw = jnp.zeros((CHUNK, 2 * HEAD_V), BF16)

    def gate_col(gtile, col):
        return jnp.sum(jnp.where(lane == col, gtile, 0.0), axis=1, keepdims=True)

    def decay_of(gcol, grow2, dirn):
        return jnp.where(incl[dirn], jnp.exp(jnp.where(incl[dirn], gcol - grow2, 0.0)), 0.0)

    def mm(a, b):
        return jnp.dot(a, b, preferred_element_type=F32)

    grp = GDN_LOCAL_GROUP

    def chunk_of(seq, g, j, dirn):
        n_chunks = seq[3]
        return grp * g + j if dirn == 0 else n_chunks - 1 - grp * g - j

    def local_stages(seq, g, slot):
        gcol_ref, grow_ref, row0, _ = seq
        box = {}

        def prep():
            sbs, rhss, dests = [], [], []
            for j in range(grp):
                for dirn in range(2):
                    c = chunk_of(seq, g, j, dirn)
                    r = pl.multiple_of(row0 + c * CHUNK, CHUNK)
                    k = ks[pl.ds(r, CHUNK), :]
                    vv = vs[pl.ds(r, CHUNK), :]
                    k16 = k.astype(BF16)
                    kk2 = _dot_nt(k16, jnp.concatenate([k16, k16], axis=0))
                    gtile = gcol_ref[0, pl.ds(pl.multiple_of(c * CHUNK, CHUNK), CHUNK), :]
                    for head in range(rep):
                        col_g = (2 * dirn) * N_V_HEADS + hq * rep + head
                        gcol = gate_col(gtile, col_g)
                        bcol = gate_col(gtile, col_g + N_V_HEADS)
                        grow2 = grow_ref[0, c, 2 * dirn, pl.ds(head, 1), :]
                        a2 = bcol * kk2 * decay_of(gcol, grow2, dirn)
                        sbs.append(jnp.where(left, eye_left, jnp.where(strict[dirn], -a2, 0.0)))
                        v = vv[:, head * HEAD_V:(head + 1) * HEAD_V]
                        rhss.append(jnp.concatenate([v * bcol, k * (bcol * jnp.exp(gcol))], axis=1).astype(BF16))
                        dests.append((dirn * rep + head, j))
            box.update(sbs=sbs, rhss=rhss, dests=dests)

        def double():
            nxt = []
            for sb in box["sbs"]:
                sb16 = sb.astype(BF16)
                nxt.append(mm(sb16, jnp.concatenate([zeros_sq, sb16], axis=0)) + jnp.where(left, sb, 0.0))
            box["sbs"] = nxt

        def apply_s32():
            box["sb16s"] = [sb.astype(BF16) for sb in box["sbs"]]
            box["xs"] = [mm(sb16, jnp.concatenate([rhs, zeros_uw], axis=0))
                         for sb16, rhs in zip(box["sb16s"], box["rhss"])]

        def finish():
            for sb16, x, (chain, j) in zip(box["sb16s"], box["xs"], box["dests"]):
                uw = x + mm(sb16, jnp.concatenate([zeros_uw, x.astype(BF16)], axis=0))
                u_s[slot, chain, j * CHUNK:(j + 1) * CHUNK, :] = uw[:, :HEAD_V]
                w_s[slot, chain, j * CHUNK:(j + 1) * CHUNK, :] = uw[:, HEAD_V:].astype(BF16)

        return [prep] + [double] * 5 + [apply_s32, finish]

    def scan_stages(seq, g, j, slot, write_out):
        gcol_ref, grow_ref, row0, _ = seq
        box = {}

        def inputs():
            work = []
            for dirn in range(2):
                c = chunk_of(seq, g, j, dirn)
                r = pl.multiple_of(row0 + c * CHUNK, CHUNK)
                lr = pl.multiple_of(c * CHUNK, CHUNK)
                k = ks[pl.ds(r, CHUNK), :]
                gtile = gcol_ref[0, pl.ds(lr, CHUNK), :]
                if write_out:
                    q = qs[pl.ds(r, CHUNK), :]
                    k16 = k.astype(BF16)
                    qk2 = _dot_nt(q, jnp.concatenate([k16, k16], axis=0))
                for head in range(rep):
                    chain = dirn * rep + head
                    gcol = gate_col(gtile, (2 * dirn) * N_V_HEADS + hq * rep + head)
                    glast = gcol[last[dirn]:last[dirn] + 1, :]
                    kdec16 = (k * jnp.exp(glast - gcol)).astype(BF16)
                    lhs = w_s[slot, chain, j * CHUNK:(j + 1) * CHUNK, :]
                    p16 = None
                    if write_out:
                        grow2 = grow_ref[0, c, 2 * dirn, pl.ds(head, 1), :]
                        p16 = (qk2 * decay_of(gcol, grow2, dirn)).astype(BF16)
                        lhs = jnp.concatenate([lhs, (q * jnp.exp(gcol)).astype(BF16)], axis=0)
                    work.append((chain, lr, head, glast, kdec16, lhs, p16))
            box["work"] = work

        def state_matmul():
            box["s_olds"] = [st[wk[0]] for wk in box["work"]]
            box["tbs"] = [mm(wk[5], s_old.astype(BF16))
                          for wk, s_old in zip(box["work"], box["s_olds"])]

        def update():
            for (chain, lr, head, glast, kdec16, _, p16), s_old, tb in zip(box["work"], box["s_olds"], box["tbs"]):
                vn16 = (u_s[slot, chain, j * CHUNK:(j + 1) * CHUNK, :] - tb[:CHUNK]).astype(BF16)
                if write_out:
                    o = tb[CHUNK:] + mm(p16, jnp.concatenate([vn16, zeros_sq], axis=0))
                    osum[pl.ds(lr, CHUNK), head * HEAD_V:(head + 1) * HEAD_V] += o
                st[chain] = s_old * jnp.exp(glast) + _dot_tn(kdec16, vn16)

        return [inputs, state_matmul, update]

    def run_group(scan_of, local_of):
        local = local_stages(*local_of) if local_of else []
        steps = [scan_stages(scan_of[0], scan_of[1], j, scan_of[2], scan_of[3]) for j in range(grp)] if scan_of else []
        pending = list(local)
        for step in steps:
            step[0]()
            for stage in step[1:]:
                if pending:
                    pending.pop(0)()
                stage()
        for stage in pending:
            stage()

    ctx_seq = (gcc_ref, gtc_ref, 0, n_ctx // CHUNK)
    lat_seq = (gc_ref, gt_ref, n_ctx, n_lat // CHUNK)
    ng_ctx, ng_lat = n_ctx // CHUNK // grp, n_lat // CHUNK // grp
    run_group(None, (ctx_seq, 0, 0))
    for g in range(ng_ctx - 1):
        run_group((ctx_seq, g, g % 2, False), (ctx_seq, g + 1, (g + 1) % 2))
    run_group((ctx_seq, ng_ctx - 1, (ng_ctx - 1) % 2, False), (lat_seq, 0, ng_ctx % 2))

    def lat_body(g, carry):
        run_group((lat_seq, g, (ng_ctx + g) % 2, True), (lat_seq, g + 1, (ng_ctx + g + 1) % 2))
        return carry

    lax.fori_loop(0, ng_lat - 1, lat_body, 0)
    run_group((lat_seq, ng_lat - 1, (ng_ctx + ng_lat - 1) % 2, True), None)

    gn = gn_ref[...]
    tile = _pick(n_lat, 256, 16)

    def epi(t, carry):
        base = pl.multiple_of(t * tile, tile)
        o = osum[pl.ds(base, tile), :]
        z = z_ref[0, pl.ds(base, tile), :].astype(F32)
        for head in range(rep):
            sl = slice(head * HEAD_V, (head + 1) * HEAD_V)
            oh = o[:, sl]
            y = oh * lax.rsqrt(jnp.mean(oh * oh, axis=-1, keepdims=True) + EPS) * gn * _silu(z[:, sl])
            y_ref[0, pl.ds(base, tile), sl] = y.astype(y_ref.dtype)
        return carry

    lax.fori_loop(0, n_lat // tile, epi, 0)


def _gdn(p_qkv, p_z, gcol, grow, pc_qkv, gcol_c, grow_c, conv_qkv, gdn_norm):
    b, n_lat, _ = p_qkv.shape
    n_ctx = pc_qkv.shape[1]
    rep = N_V_HEADS // N_QK_HEADS
    vw = rep * HEAD_V
    kq0 = QK_DIM // HEAD_K
    v0 = 2 * QK_DIM // vw
    nc_lat, nc_ctx = n_lat // CHUNK, n_ctx // CHUNK
    assert nc_lat % GDN_LOCAL_GROUP == 0 and nc_ctx % GDN_LOCAL_GROUP == 0
    assert HEAD_K == HEAD_V == N_GATE_COLS == 2 * CHUNK
    grow5 = grow.reshape(b, nc_lat, 4, N_QK_HEADS, rep, 2 * CHUNK)
    grow5_c = grow_c.reshape(b, nc_ctx, 4, N_QK_HEADS, rep, 2 * CHUNK)

    def seq_specs(n):
        return [pl.BlockSpec((1, n, HEAD_K), lambda i, h: (i, 0, h)),
                pl.BlockSpec((1, n, HEAD_K), lambda i, h: (i, 0, kq0 + h)),
                pl.BlockSpec((1, n, vw), lambda i, h: (i, 0, v0 + h))]

    def gate_specs(n, nc):
        return [pl.BlockSpec((1, n, N_GATE_COLS), lambda i, h: (i, 0, 0)),
                pl.BlockSpec((1, nc, 4, None, rep, 2 * CHUNK), lambda i, h: (i, 0, 0, h, 0, 0))]

    in_specs = (seq_specs(n_lat)
                + [pl.BlockSpec((1, n_lat, vw), lambda i, h: (i, 0, h))]
                + gate_specs(n_lat, nc_lat)
                + seq_specs(n_ctx)
                + gate_specs(n_ctx, nc_ctx)
                + [pl.BlockSpec((3, HEAD_K), lambda i, h: (0, h)),
                   pl.BlockSpec((3, HEAD_K), lambda i, h: (0, kq0 + h)),
                   pl.BlockSpec((3, vw), lambda i, h: (0, v0 + h)),
                   pl.BlockSpec((1, HEAD_V), lambda i, h: (0, 0))])
    n_all = n_ctx + n_lat
    return pl.pallas_call(
        functools.partial(_gdn_kernel, n_lat=n_lat, n_ctx=n_ctx),
        out_shape=jax.ShapeDtypeStruct((b, n_lat, V_DIM), BF16),
        grid=(b, N_QK_HEADS),
        in_specs=in_specs,
        out_specs=pl.BlockSpec((1, n_lat, vw), lambda i, h: (i, 0, h)),
        scratch_shapes=[pltpu.VMEM((n_all, HEAD_K), F32),
                        pltpu.VMEM((n_all, HEAD_K), F32),
                        pltpu.VMEM((n_all, vw), F32),
                        pltpu.VMEM((n_lat, vw), F32),
                        pltpu.VMEM((2 * rep, HEAD_K, HEAD_V), F32),
                        pltpu.VMEM((2, 2 * rep, GDN_LOCAL_GROUP * CHUNK, HEAD_V), F32),
                        pltpu.VMEM((2, 2 * rep, GDN_LOCAL_GROUP * CHUNK, HEAD_K), BF16)],
        compiler_params=_params(("parallel", "parallel")),
        name="gdn_bidir",
    )(p_qkv, p_qkv, p_qkv, p_z, gcol, grow5,
      pc_qkv, pc_qkv, pc_qkv, gcol_c, grow5_c,
      conv_qkv, conv_qkv, conv_qkv, gdn_norm)


def _conva_kernel(gb_ref, gcx_ref, xin_ref, w_ref, o_ref):
    u = gcx_ref[...].astype(F32) * xin_ref[...].astype(F32)
    rows = u.shape[0]
    pos = lax.broadcasted_iota(I32, u.shape, 0) % GRID_W
    up = jnp.where(pos == 0, 0.0, pltpu.roll(u, 1, 0))
    un = jnp.where(pos == GRID_W - 1, 0.0, pltpu.roll(u, rows - 1, 0))
    w = w_ref[...]
    conv = up * w[0:1] + u * w[1:2] + un * w[2:3]
    o_ref[...] = (gb_ref[...].astype(F32) * conv).astype(o_ref.dtype)


def _conva(p_conv, conv_w):
    t, c3 = p_conv.shape
    dc = c3 // 3
    tr = _pick(t, 512, GRID_W)
    tc = _pick(dc, 512, V7X_LANES)
    nj = dc // tc
    return pl.pallas_call(
        _conva_kernel,
        out_shape=jax.ShapeDtypeStruct((t, dc), BF16),
        grid=(t // tr, nj),
        in_specs=[pl.BlockSpec((tr, tc), lambda i, j: (i, j)),
                  pl.BlockSpec((tr, tc), lambda i, j: (i, nj + j)),
                  pl.BlockSpec((tr, tc), lambda i, j: (i, 2 * nj + j)),
                  pl.BlockSpec((3, tc), lambda i, j: (0, j))],
        out_specs=pl.BlockSpec((tr, tc), lambda i, j: (i, j)),
        compiler_params=_params(("parallel", "parallel")),
        name="mixer_a_conv",
    )(p_conv, p_conv, p_conv, conv_w)


def _mixout_kernel(ya_ref, yb_ref, ga_ref, gb_ref, wo_ref, x_ref, mod_ref, o_ref, *, gate_idx):
    mix = (_sigmoid(ga_ref[0].astype(F32)) * ya_ref[0].astype(F32)
           + _sigmoid(gb_ref[0].astype(F32)) * yb_ref[0].astype(F32))
    out = _dot(mix, wo_ref[...])
    m = mod_ref[0]
    o_ref[0] = x_ref[0] + m[gate_idx:gate_idx + 1] * out


def _mixout(ya, yb, p_gate, w_o, x, mod3):
    b, n, d = x.shape
    tm = _pick(n, 512, 16)
    return pl.pallas_call(
        functools.partial(_mixout_kernel, gate_idx=2),
        out_shape=jax.ShapeDtypeStruct((b, n, d), F32),
        grid=(b, n // tm),
        in_specs=[pl.BlockSpec((1, tm, d), lambda i, j: (i, j, 0)),
                  pl.BlockSpec((1, tm, d), lambda i, j: (i, j, 0)),
                  pl.BlockSpec((1, tm, d), lambda i, j: (i, j, 0)),
                  pl.BlockSpec((1, tm, d), lambda i, j: (i, j, 1)),
                  pl.BlockSpec((d, d), lambda i, j: (0, 0)),
                  pl.BlockSpec((1, tm, d), lambda i, j: (i, j, 0)),
                  pl.BlockSpec((1, 6, d), lambda i, j: (i, 0, 0))],
        out_specs=pl.BlockSpec((1, tm, d), lambda i, j: (i, j, 0)),
        compiler_params=_params(("parallel", "parallel")),
        name="merge_outproj",
    )(ya, yb, p_gate, p_gate, w_o, x, mod3)


def _router_kernel(x_ref, w_ref, mod_ref, wr_ref, bias_ref, hs_ref, h16_ref, idx_ref, wgt_ref, *,
                   shift_idx, scale_idx):
    x = x_ref[0]
    y = x * lax.rsqrt(jnp.mean(x * x, axis=-1, keepdims=True) + EPS) * w_ref[...]
    m = mod_ref[0]
    h = y * (1.0 + m[scale_idx:scale_idx + 1]) + m[shift_idx:shift_idx + 1]
    tm, d = h.shape
    nl = d // V7X_LANES
    for c in range(nl):
        hs_ref[0, pl.ds(c, tm, stride=nl), :] = h[:, c * V7X_LANES:(c + 1) * V7X_LANES]
    h_hi = h.astype(BF16)
    h16_ref[0] = h_hi
    h_lo = (h - h_hi.astype(F32)).astype(BF16)
    wr = wr_ref[...]
    w_hi = wr.astype(BF16)
    w_lo = (wr - w_hi.astype(F32)).astype(BF16)
    logits = _dot_nt(w_hi, h_hi) + _dot_nt(w_hi, h_lo) + _dot_nt(w_lo, h_hi)
    scores = _sigmoid(logits)
    biased = scores + bias_ref[...]
    per_group = N_EXPERTS // N_GROUPS
    neg = -jnp.inf
    i32g = lax.broadcasted_iota(I32, (per_group, tm), 0).astype(F32)
    groups, gscore = [], []
    for g in range(N_GROUPS):
        sg = biased[g * per_group:(g + 1) * per_group, :]
        m1 = jnp.max(sg, axis=0, keepdims=True)
        a1 = jnp.min(jnp.where(sg == m1, i32g, float(per_group)), axis=0, keepdims=True)
        m2 = jnp.max(jnp.where(i32g == a1, neg, sg), axis=0, keepdims=True)
        groups.append(sg)
        gscore.append(m1 + m2)
    parts = []
    for g in range(N_GROUPS):
        ahead = jnp.zeros((1, tm), F32)
        for o in range(N_GROUPS):
            if o == g:
                continue
            beats = (gscore[o] >= gscore[g]) if o < g else (gscore[o] > gscore[g])
            ahead = ahead + jnp.where(beats, 1.0, 0.0)
        parts.append(jnp.where(ahead < float(TOPK_GROUPS), groups[g], neg))
    masked = jnp.concatenate(parts, axis=0)
    ei = lax.broadcasted_iota(I32, masked.shape, 0).astype(F32)
    idxs, wgts = [], []
    for _ in range(TOP_K):
        em = jnp.max(masked, axis=0, keepdims=True)
        ea = jnp.min(jnp.where(masked == em, ei, float(N_EXPERTS)), axis=0, keepdims=True)
        hit = ei == ea
        idxs.append(ea)
        wgts.append(jnp.sum(jnp.where(hit, scores, 0.0), axis=0, keepdims=True))
        masked = jnp.where(hit, neg, masked)
    idx = jnp.concatenate(idxs, axis=0)
    wgt = jnp.concatenate(wgts, axis=0)
    wgt = wgt / jnp.sum(wgt, axis=0, keepdims=True) * ROUTED_SCALE
    idx_ref[0] = idx.astype(I32)
    wgt_ref[0] = wgt


def _router(x1, norm_w, mod3, wr_t, bias_col):
    b, n, d = x1.shape
    tm = _pick(n, 256, V7X_LANES)
    nt = n // tm
    nl = d // V7X_LANES
    return pl.pallas_call(
        functools.partial(_router_kernel, shift_idx=3, scale_idx=4),
        out_shape=(jax.ShapeDtypeStruct((b, n * nl, V7X_LANES), F32),
                   jax.ShapeDtypeStruct((b, n, d), BF16),
                   jax.ShapeDtypeStruct((b, TOP_K, n), I32),
                   jax.ShapeDtypeStruct((b, TOP_K, n), F32)),
        grid=(b, nt),
        in_specs=[pl.BlockSpec((1, tm, d), lambda i, j: (i, j, 0)),
                  pl.BlockSpec((1, d), lambda i, j: (0, 0)),
                  pl.BlockSpec((1, 6, d), lambda i, j: (i, 0, 0)),
                  pl.BlockSpec((N_EXPERTS, d), lambda i, j: (0, 0)),
                  pl.BlockSpec((N_EXPERTS, 1), lambda i, j: (0, 0))],
        out_specs=(pl.BlockSpec((1, tm * nl, V7X_LANES), lambda i, j: (i, j, 0)),
                   pl.BlockSpec((1, tm, d), lambda i, j: (i, j, 0)),
                   pl.BlockSpec((1, TOP_K, tm), lambda i, j: (i, 0, j)),
                   pl.BlockSpec((1, TOP_K, tm), lambda i, j: (i, 0, j))),
        compiler_params=_params(("parallel", "parallel")),
        name="ffn_norm_router",
    )(x1, norm_w, mod3, wr_t, bias_col)


def _spread(total, weights):
    acc, out, lo = 0.0, [], 0
    for w in weights:
        acc += w
        hi = round(total * acc / sum(weights))
        out.append(range(lo, hi))
        lo = hi
    return out


def _experts_kernel(be_ref, nu_ref, tok_ref, slot_ref, h_hbm, wg_ref, wu_ref, wd_ref, y_hbm,
                    xbuf, ybuf, gsem, ssem, *, bm):
    j = pl.program_id(0)
    n_used = nu_ref[0]
    nl = h_hbm.shape[1]
    d = nl * V7X_LANES
    de = wg_ref.shape[2]
    mxu = V7X_MXU_COLS
    pitch = EXPERT_SLAB_ROWS

    def wait_step_dmas():
        pltpu.make_async_copy(h_hbm.at[pl.ds(0, bm)], y_hbm.at[pl.ds(0, bm)], gsem).wait()
        pltpu.make_async_copy(h_hbm.at[pl.ds(0, bm)], y_hbm.at[pl.ds(0, bm)], ssem).wait()

    @pl.when(j == 0)
    def _():
        xbuf[...] = jnp.zeros(xbuf.shape, F32)
        ybuf[...] = jnp.zeros(ybuf.shape, F32)

    @pl.when((j >= 1) & (j <= n_used + 1))
    def _():
        wait_step_dmas()

    def step(nxt):
        cur = 1 - nxt

        def move_rows(rows):
            for r in rows:
                pltpu.make_async_copy(h_hbm.at[tok_ref[0, 0, r]],
                                      xbuf.at[nxt, pl.ds(r * pitch, nl)], gsem).start(priority=0)
                pltpu.make_async_copy(ybuf.at[nxt, pl.ds(r * pitch, nl)],
                                      y_hbm.at[slot_ref[0, 0, r]], ssem).start(priority=1)

        n_up, n_down = de // mxu, d // mxu
        lanes_per_slice = mxu // V7X_LANES
        parts = _spread(bm, [float(d)] * (2 * n_up) + [float(de)] * n_down)
        x = jnp.concatenate([xbuf[cur, pl.ds(c, bm, stride=pitch), :] for c in range(nl)], axis=1).astype(BF16)
        hid = []
        for s in range(n_up):
            cols = slice(s * mxu, (s + 1) * mxu)
            g = jnp.dot(x, wg_ref[0, :, cols].astype(BF16), preferred_element_type=F32)
            move_rows(parts[2 * s])
            u = jnp.dot(x, wu_ref[0, :, cols].astype(BF16), preferred_element_type=F32)
            move_rows(parts[2 * s + 1])
            hid.append((_silu(g) * u).astype(BF16))
        hid = jnp.concatenate(hid, axis=1)
        for s in range(n_down):
            cols = slice(s * mxu, (s + 1) * mxu)
            y = jnp.dot(hid, wd_ref[0, :, cols].astype(BF16), preferred_element_type=F32)
            for c in range(lanes_per_slice):
                ybuf[cur, pl.ds(s * lanes_per_slice + c, bm, stride=pitch), :] = y[:, c * V7X_LANES:(c + 1) * V7X_LANES]
            move_rows(parts[2 * n_up + s])

    for parity in range(2):
        @pl.when((j <= n_used + 1) & (j % 2 == parity))
        def _(parity=parity):
            step(parity)

    @pl.when(j == n_used + 1)
    def _():
        wait_step_dmas()


def _experts(h2, block_expert, n_used, tok_buf, slot_buf, w_gate, w_up, w_down):
    t, nl, _ = h2.shape
    d = nl * V7X_LANES
    de = w_gate.shape[2]
    n_blocks, _, bm = tok_buf.shape
    assert de % V7X_MXU_COLS == 0 and d % V7X_MXU_COLS == 0 and EXPERT_SLAB_ROWS >= nl
    last = n_blocks - 1

    def weights_of(j, be, nu):
        return (be[jnp.maximum(j - 1, 0)], 0, 0)

    grid_spec = pltpu.PrefetchScalarGridSpec(
        num_scalar_prefetch=2,
        grid=(n_blocks + 1,),
        in_specs=[
            pl.BlockSpec((1, 1, bm), lambda j, be, nu: (jnp.minimum(j, last), 0, 0), memory_space=pltpu.SMEM),
            pl.BlockSpec((1, 1, bm), lambda j, be, nu: (jnp.where(j >= 2, jnp.minimum(j - 2, last), last), 0, 0),
                         memory_space=pltpu.SMEM),
            pl.BlockSpec(memory_space=pl.ANY),
            pl.BlockSpec((1, d, de), weights_of),
            pl.BlockSpec((1, d, de), weights_of),
            pl.BlockSpec((1, de, d), weights_of),
        ],
        out_specs=pl.BlockSpec(memory_space=pl.ANY),
        scratch_shapes=[pltpu.VMEM((2, bm * EXPERT_SLAB_ROWS, V7X_LANES), F32),
                        pltpu.VMEM((2, bm * EXPERT_SLAB_ROWS, V7X_LANES), F32),
                        pltpu.SemaphoreType.DMA,
                        pltpu.SemaphoreType.DMA],
    )
    return pl.pallas_call(
        functools.partial(_experts_kernel, bm=bm),
        out_shape=jax.ShapeDtypeStruct((TOP_K * t + bm, nl, V7X_LANES), F32),
        grid_spec=grid_spec,
        compiler_params=_params(("arbitrary",)),
        name="routed_experts",
    )(block_expert, n_used, tok_buf, slot_buf, h2, w_gate, w_up, w_down)


def _shared_kernel(h_ref, wg_ref, wu_ref, wd_ref, o_ref):
    x = h_ref[...]
    hid = _silu(jnp.dot(x, wg_ref[...], preferred_element_type=F32)) * jnp.dot(
        x, wu_ref[...], preferred_element_type=F32)
    o_ref[...] = jnp.dot(hid.astype(BF16), wd_ref[...], preferred_element_type=F32).astype(o_ref.dtype)


def _shared(h2, wg, wu, wd):
    t, d = h2.shape
    ds = wg.shape[1]
    tm = _pick(t, 512, 16)
    return pl.pallas_call(
        _shared_kernel,
        out_shape=jax.ShapeDtypeStruct((t, d), F32),
        grid=(t // tm,),
        in_specs=[pl.BlockSpec((tm, d), lambda i: (i, 0)),
                  pl.BlockSpec((d, ds), lambda i: (0, 0)),
                  pl.BlockSpec((d, ds), lambda i: (0, 0)),
                  pl.BlockSpec((ds, d), lambda i: (0, 0))],
        out_specs=pl.BlockSpec((tm, d), lambda i: (i, 0)),
        compiler_params=_params(("parallel",)),
        name="shared_expert",
    )(h2, wg, wu, wd)


def _combine_kernel(x_ref, wt_ref, sh_ref, mod_ref, fw_ref, *rest, gate_idx):
    yk_refs, o_ref = rest[:TOP_K], rest[TOP_K]
    wt = wt_ref[0]
    sh = sh_ref[0]
    tm, d = sh.shape
    nl = d // V7X_LANES
    cols = []
    for c in range(nl):
        acc = sh[:, c * V7X_LANES:(c + 1) * V7X_LANES]
        for kk in range(TOP_K):
            acc = acc + wt[:, kk:kk + 1] * yk_refs[kk][pl.ds(c, tm, stride=nl), :]
        cols.append(acc)
    acc = jnp.concatenate(cols, axis=1)
    m = mod_ref[0]
    x = x_ref[0] + m[gate_idx:gate_idx + 1] * acc
    o_ref[0] = x * lax.rsqrt(jnp.mean(x * x, axis=-1, keepdims=True) + EPS) * fw_ref[...]


def _combine(x1, yk, wtok, shared, mod3, final_w):
    b, n, d = x1.shape
    tm = _pick(n, 128, 16)
    nt = n // tm
    nl = d // V7X_LANES

    def yk_spec(kk):
        return pl.BlockSpec((tm * nl, V7X_LANES), lambda i, j: ((kk * b + i) * nt + j, 0))

    return pl.pallas_call(
        functools.partial(_combine_kernel, gate_idx=5),
        out_shape=jax.ShapeDtypeStruct((b, n, d), F32),
        grid=(b, nt),
        in_specs=[pl.BlockSpec((1, tm, d), lambda i, j: (i, j, 0)),
                  pl.BlockSpec((1, tm, TOP_K), lambda i, j: (i, j, 0)),
                  pl.BlockSpec((1, tm, d), lambda i, j: (i, j, 0)),
                  pl.BlockSpec((1, 6, d), lambda i, j: (i, 0, 0)),
                  pl.BlockSpec((1, d), lambda i, j: (0, 0))] + [yk_spec(kk) for kk in range(TOP_K)],
        out_specs=pl.BlockSpec((1, tm, d), lambda i, j: (i, j, 0)),
        compiler_params=_params(("parallel", "parallel")),
        name="combine_final_norm",
    )(x1, wtok, shared, mod3, final_w, *([yk] * TOP_K))


def _routing_tables(idx_kt, n_tokens):
    bm = EXPERT_BLOCK_ROWS
    n_assign = TOP_K * n_tokens
    assert n_assign % bm == 0
    n_blocks = n_assign // bm + N_EXPERTS
    e_flat = idx_kt.reshape(n_assign)
    experts = jnp.arange(N_EXPERTS, dtype=I32)
    counts = jnp.sum((e_flat[None, :] == experts[:, None]).astype(I32), axis=1)
    n_pad = (-counts) % bm
    big = 2 * N_EXPERTS
    fill = jnp.arange(bm, dtype=I32)[None, :] < n_pad[:, None]
    fill_keys = jnp.where(fill, 2 * experts[:, None] + 1, big).reshape(N_EXPERTS * bm)
    keys = jnp.concatenate([2 * e_flat, fill_keys])
    vals = jnp.concatenate([jnp.arange(n_assign, dtype=I32), jnp.full((N_EXPERTS * bm,), -1, I32)])
    keys, vals = lax.sort((keys, vals), num_keys=1)
    valid = vals >= 0
    tok = jnp.where(valid, vals % n_tokens, 0)
    slot = jnp.where(valid, vals, n_assign + jnp.arange(n_blocks * bm, dtype=I32) % bm)
    n_used = (jnp.sum(counts + n_pad) // bm).astype(I32)
    blk = jnp.minimum(jnp.arange(n_blocks, dtype=I32), n_used - 1)
    block_expert = keys.reshape(n_blocks, bm)[:, 0][blk] // 2
    return block_expert, n_used.reshape(1), tok.reshape(n_blocks, 1, bm), slot.reshape(n_blocks, 1, bm)


def kernel(x, c, ctx, c_ctx, w_ada, b_ada, norm_mix, norm_ffn, w_in, conv_a, conv_qkv, a_log, dt_bias,
           gdn_norm, w_proj_a, w_proj_b, w_out, w_router, router_bias, w_gate, w_up, w_down,
           ws_gate, ws_up, ws_down, final_norm):
    assert w_ada.shape[0] == 1, "single (final) layer only"
    b, seq, d = x.shape
    n_ctx = ctx.shape[1]
    t = b * seq
    dc = conv_a.shape[2]
    assert seq % GRID_W == 0 and seq % CHUNK == 0 and n_ctx % CHUNK == 0
    z_off = QKV_DIM + N_GATE_COLS
    conv_off = z_off + V_DIM
    gate_off = conv_off + 3 * dc
    assert w_in.shape[2] == gate_off + 2 * d

    rows = -(-(b + 1) // 8) * 8
    cpad = jnp.zeros((rows, d), F32).at[:b].set(c).at[b].set(c_ctx)
    mod3 = _ada(cpad, w_ada[0], b_ada).reshape(rows, 6, d)

    h = _norm_mod(x, norm_mix, mod3, lambda i: i, 0, 1, BF16).reshape(t, d)
    hc = _norm_mod(ctx, norm_mix, mod3, lambda i: b, 0, 1, BF16).reshape(b * n_ctx, d)
    w_in0 = w_in[0]
    w_qkv = w_in0[:, :QKV_DIM].astype(BF16)
    w_ab = w_in0[:, QKV_DIM:z_off].astype(BF16)
    p_qkv = _mm(h, w_qkv, BF16, "inproj_qkv").reshape(b, seq, QKV_DIM)
    p_ab = _mm(h, w_ab, F32, "inproj_ab").reshape(b, seq, N_GATE_COLS)
    p_z = _mm(h, w_in0[:, z_off:conv_off].astype(BF16), BF16, "inproj_z").reshape(b, seq, V_DIM)
    p_conv = _mm(h, w_in0[:, conv_off:gate_off].astype(BF16), BF16, "inproj_conv")
    p_gate = _mm(h, w_in0[:, gate_off:].astype(BF16), BF16, "inproj_gate").reshape(b, seq, 2 * d)
    pc_qkv = _mm(hc, w_qkv, BF16, "inproj_qkv_ctx").reshape(b, n_ctx, QKV_DIM)
    pc_ab = _mm(hc, w_ab, F32, "inproj_ab_ctx").reshape(b, n_ctx, N_GATE_COLS)

    zero = jnp.zeros((N_V_HEADS,), F32)
    prm = jnp.stack([jnp.concatenate([a_log[0, 0], zero, a_log[0, 1], zero]),
                     jnp.concatenate([dt_bias[0, 0], zero, dt_bias[0, 1], zero])])
    gcol, grow = _gates(p_ab, prm)
    gcol_c, grow_c = _gates(pc_ab, prm)
    yb_in = _gdn(p_qkv, p_z, gcol, grow, pc_qkv, gcol_c, grow_c, conv_qkv[0], gdn_norm)

    ya_in = _conva(p_conv, conv_a[0])
    ya = _mm(ya_in, w_proj_a[0].astype(BF16), BF16, "proj_a").reshape(b, seq, d)
    yb = _mm(yb_in.reshape(t, V_DIM), w_proj_b[0].astype(BF16), BF16, "proj_b").reshape(b, seq, d)
    x1 = _mixout(ya, yb, p_gate, w_out[0].astype(BF16), x, mod3)

    nl = d // V7X_LANES
    h2_slabs, h2_16, idx_bkt, wgt_bkt = _router(x1, norm_ffn, mod3, w_router[0].T,
                                                router_bias[0].reshape(N_EXPERTS, 1))
    idx_kt = jnp.transpose(idx_bkt, (1, 0, 2)).reshape(TOP_K, t)
    wtok = jnp.transpose(wgt_bkt, (0, 2, 1))
    block_expert, n_used, tok_buf, slot_buf = _routing_tables(idx_kt, t)
    yk = _experts(h2_slabs.reshape(t, nl, V7X_LANES), block_expert, n_used, tok_buf, slot_buf,
                  w_gate[0], w_up[0], w_down[0])
    shared = _shared(h2_16.reshape(t, d), ws_gate[0].astype(BF16), ws_up[0].astype(BF16),
                     ws_down[0].astype(BF16)).reshape(b, seq, d)
    return _combine(x1, yk.reshape(-1, V7X_LANES), wtok, shared, mod3, final_norm.reshape(1, d))
```

```python
import functools

import jax
import jax.numpy as jnp
from jax import lax
from jax.experimental import pallas as pl
from jax.experimental.pallas import tpu as pltpu

F32 = jnp.float32
BF16 = jnp.bfloat16
I32 = jnp.int32

EPS = 1e-6
GRID_W = 64
N_QK_HEADS = 16
N_V_HEADS = 32
HEAD_K = 128
HEAD_V = 128
CHUNK = 64
N_EXPERTS = 256
N_GROUPS = 8
TOPK_GROUPS = 4
TOP_K = 8
ROUTED_SCALE = 2.5

QK_DIM = N_QK_HEADS * HEAD_K
V_DIM = N_V_HEADS * HEAD_V
QKV_DIM = 2 * QK_DIM + V_DIM
N_GATE_COLS = 4 * N_V_HEADS

V7X_LANES = 128
V7X_VMEM_LIMIT_BYTES = 56 * 1024 * 1024
V7X_MXU_COLS = 256
EXPERT_BLOCK_ROWS = 256
EXPERT_SLAB_ROWS = 9
GDN_LOCAL_GROUP = 4


def _pick(n, target, mult):
    if n <= target:
        return n
    t = (target // mult) * mult
    while t >= mult:
        if n % t == 0:
            return t
        t -= mult
    raise ValueError(f"no tile for {n} (target {target}, multiple {mult})")


def _params(sem, vmem=V7X_VMEM_LIMIT_BYTES):
    return pltpu.CompilerParams(dimension_semantics=sem, vmem_limit_bytes=vmem)


def _sigmoid(x):
    return 1.0 / (1.0 + jnp.exp(-x))


def _silu(x):
    return x * _sigmoid(x)


def _dot(a, b):
    return jnp.dot(a.astype(BF16), b.astype(BF16), preferred_element_type=F32)


def _dot_nt(a, b):
    return lax.dot_general(a.astype(BF16), b.astype(BF16), (((1,), (1,)), ((), ())),
                           preferred_element_type=F32)


def _dot_tn(a, b):
    return lax.dot_general(a.astype(BF16), b.astype(BF16), (((0,), (0,)), ((), ())),
                           preferred_element_type=F32)


def _pack_halves(x):
    n = x.shape[1] // 2
    bits = lax.bitcast_convert_type(x.astype(BF16).astype(F32), jnp.uint32)
    return bits[:, n:] | lax.shift_right_logical(bits[:, :n], jnp.uint32(16))


def _unpack_halves(w):
    lo = lax.bitcast_convert_type(lax.shift_left(w, jnp.uint32(16)), F32)
    hi = lax.bitcast_convert_type(w & jnp.uint32(0xFFFF0000), F32)
    return lo, hi


def _split3(x):
    hi = x.astype(BF16)
    r = x - hi.astype(F32)
    mid = r.astype(BF16)
    lo = (r - mid.astype(F32)).astype(BF16)
    return hi, mid, lo


def _ada_kernel(c_ref, w_ref, b_ref, o_ref):
    a = _silu(c_ref[...])
    o_ref[...] = _dot(a, w_ref[...]) + b_ref[...]


def _ada(cpad, w, b):
    rows, d = cpad.shape
    n = w.shape[1]
    tn = _pick(n, 1024, V7X_LANES)
    return pl.pallas_call(
        _ada_kernel,
        out_shape=jax.ShapeDtypeStruct((rows, n), F32),
        grid=(n // tn,),
        in_specs=[pl.BlockSpec((rows, d), lambda j: (0, 0)),
                  pl.BlockSpec((d, tn), lambda j: (0, j)),
                  pl.BlockSpec((1, tn), lambda j: (0, j))],
        out_specs=pl.BlockSpec((rows, tn), lambda j: (0, j)),
        compiler_params=_params(("parallel",)),
        name="ada_mod",
    )(cpad, w, b)


def _norm_mod_kernel(x_ref, w_ref, mod_ref, o_ref, *, shift_idx, scale_idx):
    x = x_ref[0]
    y = x * lax.rsqrt(jnp.mean(x * x, axis=-1, keepdims=True) + EPS) * w_ref[...]
    m = mod_ref[0]
    o_ref[0] = (y * (1.0 + m[scale_idx:scale_idx + 1]) + m[shift_idx:shift_idx + 1]).astype(o_ref.dtype)


def _norm_mod(x, w, mod3, row_of_batch, shift_idx, scale_idx, out_dtype):
    b, n, d = x.shape
    ts = _pick(n, 512, 16)
    return pl.pallas_call(
        functools.partial(_norm_mod_kernel, shift_idx=shift_idx, scale_idx=scale_idx),
        out_shape=jax.ShapeDtypeStruct((b, n, d), out_dtype),
        grid=(b, n // ts),
        in_specs=[pl.BlockSpec((1, ts, d), lambda i, j: (i, j, 0)),
                  pl.BlockSpec((1, d), lambda i, j: (0, 0)),
                  pl.BlockSpec((1, 6, d), lambda i, j: (row_of_batch(i), 0, 0))],
        out_specs=pl.BlockSpec((1, ts, d), lambda i, j: (i, j, 0)),
        compiler_params=_params(("parallel", "parallel")),
        name="norm_mod",
    )(x, w, mod3)


def _mm_kernel(a_ref, w_ref, o_ref):
    o_ref[...] = jnp.dot(a_ref[...], w_ref[...], preferred_element_type=F32).astype(o_ref.dtype)


def _mm(a, w, out_dtype, name):
    m, k = a.shape
    n = w.shape[1]
    tm = _pick(m, 1024, 16)
    tn = _pick(n, 1024, V7X_LANES)
    return pl.pallas_call(
        _mm_kernel,
        out_shape=jax.ShapeDtypeStruct((m, n), out_dtype),
        grid=(n // tn, m // tm),
        in_specs=[pl.BlockSpec((tm, k), lambda j, i: (i, 0)),
                  pl.BlockSpec((k, tn), lambda j, i: (0, j))],
        out_specs=pl.BlockSpec((tm, tn), lambda j, i: (i, j)),
        compiler_params=_params(("parallel", "parallel")),
        name=name,
    )(a, w)


def _gates_kernel(ab_ref, prm_ref, col_ref, row_ref, *, rows):
    sub = 2 * CHUNK
    n_sub = rows // sub
    prm = prm_ref[...]
    alog = prm[0:1]
    dtb = prm[1:2]
    lane = lax.broadcasted_iota(I32, (sub, N_GATE_COLS), 1)
    is_g = (lane // N_V_HEADS) % 2 == 0
    is_bwd = lane >= 2 * N_V_HEADS
    r = lax.broadcasted_iota(I32, (sub, sub), 0)
    c = lax.broadcasted_iota(I32, (sub, sub), 1)
    same = (r // CHUNK) == (c // CHUNK)
    cum_f = jnp.where(same & (c <= r), 1.0, 0.0).astype(BF16)
    cum_b = jnp.where(same & (c >= r), 1.0, 0.0).astype(BF16)
    for s in range(n_sub):
        x = ab_ref[0, s * sub:(s + 1) * sub, :]
        xs = x + dtb
        softplus = jnp.maximum(xs, 0.0) + jnp.log(1.0 + jnp.exp(-jnp.abs(xs)))
        g = -jnp.exp(alog) * softplus
        beta = _sigmoid(x)
        hi, mid, lo = _split3(g)
        pf = (jnp.dot(cum_f, hi, preferred_element_type=F32) + jnp.dot(cum_f, mid, preferred_element_type=F32)
              + jnp.dot(cum_f, lo, preferred_element_type=F32))
        pb = (jnp.dot(cum_b, hi, preferred_element_type=F32) + jnp.dot(cum_b, mid, preferred_element_type=F32)
              + jnp.dot(cum_b, lo, preferred_element_type=F32))
        out = jnp.where(is_g, jnp.where(is_bwd, pb, pf), beta)
        col_ref[0, s * sub:(s + 1) * sub, :] = out
        for h in range(2):
            oc = out[h * CHUNK:(h + 1) * CHUNK]
            row_ref[0, 2 * s + h] = jnp.concatenate([oc, oc], axis=0).T


def _gates(ab, prm):
    b, n, g = ab.shape
    rows = _pick(n, 512, 2 * CHUNK)
    return pl.pallas_call(
        functools.partial(_gates_kernel, rows=rows),
        out_shape=(jax.ShapeDtypeStruct((b, n, g), F32),
                   jax.ShapeDtypeStruct((b, n // CHUNK, g, 2 * CHUNK), F32)),
        grid=(b, n // rows),
        in_specs=[pl.BlockSpec((1, rows, g), lambda i, j: (i, j, 0)),
                  pl.BlockSpec((2, g), lambda i, j: (0, 0))],
        out_specs=(pl.BlockSpec((1, rows, g), lambda i, j: (i, j, 0)),
                   pl.BlockSpec((1, rows // CHUNK, g, 2 * CHUNK), lambda i, j: (i, j, 0, 0))),
        compiler_params=_params(("parallel", "parallel")),
        name="gate_prep",
    )(ab, prm)


def _conv_silu_tile(src_ref, w, base, rows, n_total):
    x = src_ref[0, pl.ds(base, rows), :].astype(F32)
    pb = pl.multiple_of(jnp.maximum(base - 16, 0), 16)
    nb = pl.multiple_of(jnp.minimum(base + rows, n_total - 16), 16)
    prev = src_ref[0, pl.ds(pb, 16), :].astype(F32)[15:16]
    nxt = src_ref[0, pl.ds(nb, 16), :].astype(F32)[0:1]
    prev = jnp.where(base > 0, prev, 0.0)
    nxt = jnp.where(base + rows < n_total, nxt, 0.0)
    ridx = lax.broadcasted_iota(I32, x.shape, 0)
    xp = jnp.where(ridx == 0, prev, pltpu.roll(x, 1, 0))
    xn = jnp.where(ridx == rows - 1, nxt, pltpu.roll(x, rows - 1, 0))
    y = xp * w[0:1] + x * w[1:2] + xn * w[2:3]
    return _silu(y)


def _l2norm_rows(t):
    return t * lax.rsqrt(jnp.sum(t * t, axis=-1, keepdims=True) + EPS)


def _gdn_kernel(q_ref, k_ref, v_ref, z_ref, gc_ref, gt_ref,
                qc_ref, kc_ref, vc_ref, gcc_ref, gtc_ref,
                cq_ref, ck_ref, cv_ref, gn_ref,
                y_ref,
                qs, ks, vs, osum, st, u_s, w_s, *, n_lat, n_ctx):
    hq = pl.program_id(1)
    rep = N_V_HEADS // N_QK_HEADS
    wq = cq_ref[...]
    wk = ck_ref[...]
    wv = cv_ref[...]

    def prep(src_q, src_k, src_v, n_total, row0):
        tile = _pick(n_total, 256, 16)

        def body(t, carry):
            base = pl.multiple_of(t * tile, tile)
            dst = pl.multiple_of(row0 + base, 16)
            qs[pl.ds(dst, tile), :] = _l2norm_rows(_conv_silu_tile(src_q, wq, base, tile, n_total)) * (HEAD_K ** -0.5)
            ks[pl.ds(dst, tile), :] = _l2norm_rows(_conv_silu_tile(src_k, wk, base, tile, n_total))
            vs[pl.ds(dst, tile), :] = _conv_silu_tile(src_v, wv, base, tile, n_total)
            return carry

        lax.fori_loop(0, n_total // tile, body, 0)

    prep(qc_ref, kc_ref, vc_ref, n_ctx, 0)
    prep(q_ref, k_ref, v_ref, n_lat, n_ctx)

    st[...] = jnp.zeros(st.shape, F32)
    osum[...] = jnp.zeros(osum.shape, F32)

    wide = (CHUNK, 2 * CHUNK)
    ri = lax.broadcasted_iota(I32, wide, 0)
    li = lax.broadcasted_iota(I32, wide, 1)
    ci = li % CHUNK
    left = li < CHUNK
    eye_left = jnp.where(left & (ri == ci), 1.0, 0.0)
    incl = (ri >= ci, ri <= ci)
    strict = (ri > ci, ri < ci)
    last = (CHUNK - 1, 0)
    lane = lax.broadcasted_iota(I32, (CHUNK, N_GATE_COLS), 1)
    zeros_sq = jnp.zeros(wide, BF16)
    zeros_uw = jnp.zeros((CHUNK, 2 * HEAD_V), BF16)

    def gate_col(gtile, col):
        return jnp.sum(jnp.where(lane == col, gtile, 0.0), axis=1, keepdims=True)

    def decay_of(gcol, grow2, dirn):
        return jnp.where(incl[dirn], jnp.exp(jnp.where(incl[dirn], gcol - grow2, 0.0)), 0.0)

    def mm(a, b):
        return jnp.dot(a, b, preferred_element_type=F32)

    grp = GDN_LOCAL_GROUP

    def chunk_of(seq, g, j, dirn):
        n_chunks = seq[3]
        return grp * g + j if dirn == 0 else n_chunks - 1 - grp * g - j

    def local_stages(seq, g, slot):
        gcol_ref, grow_ref, row0, _ = seq
        box = {}

        def prep():
            sbs, rhss, dests = [], [], []
            for j in range(grp):
                for dirn in range(2):
                    c = chunk_of(seq, g, j, dirn)
                    r = pl.multiple_of(row0 + c * CHUNK, CHUNK)
                    k = ks[pl.ds(r, CHUNK), :]
                    vv = vs[pl.ds(r, CHUNK), :]
                    k16 = k.astype(BF16)
                    kk2 = _dot_nt(k16, jnp.concatenate([k16, k16], axis=0))
                    gtile = gcol_ref[0, pl.ds(pl.multiple_of(c * CHUNK, CHUNK), CHUNK), :]
                    for head in range(rep):
                        col_g = (2 * dirn) * N_V_HEADS + hq * rep + head
                        gcol = gate_col(gtile, col_g)
                        bcol = gate_col(gtile, col_g + N_V_HEADS)
                        grow2 = grow_ref[0, c, 2 * dirn, pl.ds(head, 1), :]
                        a2 = bcol * kk2 * decay_of(gcol, grow2, dirn)
                        sbs.append(jnp.where(left, eye_left, jnp.where(strict[dirn], -a2, 0.0)))
                        v = vv[:, head * HEAD_V:(head + 1) * HEAD_V]
                        rhss.append(jnp.concatenate([v * bcol, k * (bcol * jnp.exp(gcol))], axis=1).astype(BF16))
                        dests.append((dirn * rep + head, j))
            box.update(sbs=sbs, rhss=rhss, dests=dests)

        def double():
            nxt = []
            for sb in box["sbs"]:
                sb16 = sb.astype(BF16)
                nxt.append(mm(sb16, jnp.concatenate([zeros_sq, sb16], axis=0)) + jnp.where(left, sb, 0.0))
            box["sbs"] = nxt

        def apply_s32():
            box["sb16s"] = [sb.astype(BF16) for sb in box["sbs"]]
            box["xs"] = [mm(sb16, jnp.concatenate([rhs, zeros_uw], axis=0))
                         for sb16, rhs in zip(box["sb16s"], box["rhss"])]

        def finish():
            for sb16, x, (chain, j) in zip(box["sb16s"], box["xs"], box["dests"]):
                uw = x + mm(sb16, jnp.concatenate([zeros_uw, x.astype(BF16)], axis=0))
                u_s[slot, chain, j * CHUNK:(j + 1) * CHUNK, :] = uw[:, :HEAD_V]
                w_s[slot, chain, j * CHUNK:(j + 1) * CHUNK, :] = uw[:, HEAD_V:].astype(BF16)

        return [prep] + [double] * 5 + [apply_s32, finish]

    def scan_stages(seq, g, j, slot, write_out):
        gcol_ref, grow_ref, row0, _ = seq
        box = {}

        def inputs():
            work = []
            for dirn in range(2):
                c = chunk_of(seq, g, j, dirn)
                r = pl.multiple_of(row0 + c * CHUNK, CHUNK)
                lr = pl.multiple_of(c * CHUNK, CHUNK)
                k = ks[pl.ds(r, CHUNK), :]
                gtile = gcol_ref[0, pl.ds(lr, CHUNK), :]
                if write_out:
                    q = qs[pl.ds(r, CHUNK), :]
                    k16 = k.astype(BF16)
                    qk2 = _dot_nt(q, jnp.concatenate([k16, k16], axis=0))
                for head in range(rep):
                    chain = dirn * rep + head
                    gcol = gate_col(gtile, (2 * dirn) * N_V_HEADS + hq * rep + head)
                    glast = gcol[last[dirn]:last[dirn] + 1, :]
                    kdec16 = (k * jnp.exp(glast - gcol)).astype(BF16)
                    lhs = w_s[slot, chain, j * CHUNK:(j + 1) * CHUNK, :]
                    p16 = None
                    if write_out:
                        grow2 = grow_ref[0, c, 2 * dirn, pl.ds(head, 1), :]
                        p16 = (qk2 * decay_of(gcol, grow2, dirn)).astype(BF16)
                        lhs = jnp.concatenate([lhs, (q * jnp.exp(gcol)).astype(BF16)], axis=0)
                    work.append((chain, lr, head, glast, kdec16, lhs, p16))
            box["work"] = work

        def state_matmul():
            box["s_olds"] = [st[wk[0]] for wk in box["work"]]
            box["tbs"] = [mm(wk[5], s_old.astype(BF16))
                          for wk, s_old in zip(box["work"], box["s_olds"])]

        def update():
            for (chain, lr, head, glast, kdec16, _, p16), s_old, tb in zip(box["work"], box["s_olds"], box["tbs"]):
                vn16 = (u_s[slot, chain, j * CHUNK:(j + 1) * CHUNK, :] - tb[:CHUNK]).astype(BF16)
                if write_out:
                    o = tb[CHUNK:] + mm(p16, jnp.concatenate([vn16, zeros_sq], axis=0))
                    osum[pl.ds(lr, CHUNK), head * HEAD_V:(head + 1) * HEAD_V] += o
                st[chain] = s_old * jnp.exp(glast) + _dot_tn(kdec16, vn16)

        return [inputs, state_matmul, update]

    def run_group(scan_of, local_of):
        local = local_stages(*local_of) if local_of else []
        steps = [scan_stages(scan_of[0], scan_of[1], j, scan_of[2], scan_of[3]) for j in range(grp)] if scan_of else []
        pending = list(local)
        for step in steps:
            step[0]()
            for stage in step[1:]:
                if pending:
                    pending.pop(0)()
                stage()
        for stage in pending:
            stage()

    ctx_seq = (gcc_ref, gtc_ref, 0, n_ctx // CHUNK)
    lat_seq = (gc_ref, gt_ref, n_ctx, n_lat // CHUNK)
    ng_ctx, ng_lat = n_ctx // CHUNK // grp, n_lat // CHUNK // grp
    run_group(None, (ctx_seq, 0, 0))
    for g in range(ng_ctx - 1):
        run_group((ctx_seq, g, g % 2, False), (ctx_seq, g + 1, (g + 1) % 2))
    run_group((ctx_seq, ng_ctx - 1, (ng_ctx - 1) % 2, False), (lat_seq, 0, ng_ctx % 2))

    def lat_body(g, carry):
        run_group((lat_seq, g, (ng_ctx + g) % 2, True), (lat_seq, g + 1, (ng_ctx + g + 1) % 2))
        return carry

    lax.fori_loop(0, ng_lat - 1, lat_body, 0)
    run_group((lat_seq, ng_lat - 1, (ng_ctx + ng_lat - 1) % 2, True), None)

    gn = gn_ref[...]
    tile = _pick(n_lat, 256, 16)

    def epi(t, carry):
        base = pl.multiple_of(t * tile, tile)
        o = osum[pl.ds(base, tile), :]
        z = z_ref[0, pl.ds(base, tile), :].astype(F32)
        for head in range(rep):
            sl = slice(head * HEAD_V, (head + 1) * HEAD_V)
            oh = o[:, sl]
            y = oh * lax.rsqrt(jnp.mean(oh * oh, axis=-1, keepdims=True) + EPS) * gn * _silu(z[:, sl])
            y_ref[0, pl.ds(base, tile), sl] = y.astype(y_ref.dtype)
        return carry

    lax.fori_loop(0, n_lat // tile, epi, 0)


def _gdn(p_qkv, p_z, gcol, grow, pc_qkv, gcol_c, grow_c, conv_qkv, gdn_norm):
    b, n_lat, _ = p_qkv.shape
    n_ctx = pc_qkv.shape[1]
    rep = N_V_HEADS // N_QK_HEADS
    vw = rep * HEAD_V
    kq0 = QK_DIM // HEAD_K
    v0 = 2 * QK_DIM // vw
    nc_lat, nc_ctx = n_lat // CHUNK, n_ctx // CHUNK
    assert nc_lat % GDN_LOCAL_GROUP == 0 and nc_ctx % GDN_LOCAL_GROUP == 0
    assert HEAD_K == HEAD_V == N_GATE_COLS == 2 * CHUNK
    grow5 = grow.reshape(b, nc_lat, 4, N_QK_HEADS, rep, 2 * CHUNK)
    grow5_c = grow_c.reshape(b, nc_ctx, 4, N_QK_HEADS, rep, 2 * CHUNK)

    def seq_specs(n):
        return [pl.BlockSpec((1, n, HEAD_K), lambda i, h: (i, 0, h)),
                pl.BlockSpec((1, n, HEAD_K), lambda i, h: (i, 0, kq0 + h)),
                pl.BlockSpec((1, n, vw), lambda i, h: (i, 0, v0 + h))]

    def gate_specs(n, nc):
        return [pl.BlockSpec((1, n, N_GATE_COLS), lambda i, h: (i, 0, 0)),
                pl.BlockSpec((1, nc, 4, None, rep, 2 * CHUNK), lambda i, h: (i, 0, 0, h, 0, 0))]

    in_specs = (seq_specs(n_lat)
                + [pl.BlockSpec((1, n_lat, vw), lambda i, h: (i, 0, h))]
                + gate_specs(n_lat, nc_lat)
                + seq_specs(n_ctx)
                + gate_specs(n_ctx, nc_ctx)
                + [pl.BlockSpec((3, HEAD_K), lambda i, h: (0, h)),
                   pl.BlockSpec((3, HEAD_K), lambda i, h: (0, kq0 + h)),
                   pl.BlockSpec((3, vw), lambda i, h: (0, v0 + h)),
                   pl.BlockSpec((1, HEAD_V), lambda i, h: (0, 0))])
    n_all = n_ctx + n_lat
    return pl.pallas_call(
        functools.partial(_gdn_kernel, n_lat=n_lat, n_ctx=n_ctx),
        out_shape=jax.ShapeDtypeStruct((b, n_lat, V_DIM), BF16),
        grid=(b, N_QK_HEADS),
        in_specs=in_specs,
        out_specs=pl.BlockSpec((1, n_lat, vw), lambda i, h: (i, 0, h)),
        scratch_shapes=[pltpu.VMEM((n_all, HEAD_K), F32),
                        pltpu.VMEM((n_all, HEAD_K), F32),
                        pltpu.VMEM((n_all, vw), F32),
                        pltpu.VMEM((n_lat, vw), F32),
                        pltpu.VMEM((2 * rep, HEAD_K, HEAD_V), F32),
                        pltpu.VMEM((2, 2 * rep, GDN_LOCAL_GROUP * CHUNK, HEAD_V), F32),
                        pltpu.VMEM((2, 2 * rep, GDN_LOCAL_GROUP * CHUNK, HEAD_K), BF16)],
        compiler_params=_params(("parallel", "parallel")),
        name="gdn_bidir",
    )(p_qkv, p_qkv, p_qkv, p_z, gcol, grow5,
      pc_qkv, pc_qkv, pc_qkv, gcol_c, grow5_c,
      conv_qkv, conv_qkv, conv_qkv, gdn_norm)


def _conva_kernel(gb_ref, gcx_ref, xin_ref, w_ref, o_ref):
    u = gcx_ref[...].astype(F32) * xin_ref[...].astype(F32)
    rows = u.shape[0]
    pos = lax.broadcasted_iota(I32, u.shape, 0) % GRID_W
    up = jnp.where(pos == 0, 0.0, pltpu.roll(u, 1, 0))
    un = jnp.where(pos == GRID_W - 1, 0.0, pltpu.roll(u, rows - 1, 0))
    w = w_ref[...]
    conv = up * w[0:1] + u * w[1:2] + un * w[2:3]
    o_ref[...] = (gb_ref[...].astype(F32) * conv).astype(o_ref.dtype)


def _conva(p_conv, conv_w):
    t, c3 = p_conv.shape
    dc = c3 // 3
    tr = _pick(t, 512, GRID_W)
    tc = _pick(dc, 512, V7X_LANES)
    nj = dc // tc
    return pl.pallas_call(
        _conva_kernel,
        out_shape=jax.ShapeDtypeStruct((t, dc), BF16),
        grid=(t // tr, nj),
        in_specs=[pl.BlockSpec((tr, tc), lambda i, j: (i, j)),
                  pl.BlockSpec((tr, tc), lambda i, j: (i, nj + j)),
                  pl.BlockSpec((tr, tc), lambda i, j: (i, 2 * nj + j)),
                  pl.BlockSpec((3, tc), lambda i, j: (0, j))],
        out_specs=pl.BlockSpec((tr, tc), lambda i, j: (i, j)),
        compiler_params=_params(("parallel", "parallel")),
        name="mixer_a_conv",
    )(p_conv, p_conv, p_conv, conv_w)


def _mixout_kernel(ya_ref, yb_ref, ga_ref, gb_ref, wo_ref, x_ref, mod_ref, o_ref, *, gate_idx):
    mix = (_sigmoid(ga_ref[0].astype(F32)) * ya_ref[0].astype(F32)
           + _sigmoid(gb_ref[0].astype(F32)) * yb_ref[0].astype(F32))
    out = _dot(mix, wo_ref[...])
    m = mod_ref[0]
    o_ref[0] = x_ref[0] + m[gate_idx:gate_idx + 1] * out


def _mixout(ya, yb, p_gate, w_o, x, mod3):
    b, n, d = x.shape
    tm = _pick(n, 512, 16)
    return pl.pallas_call(
        functools.partial(_mixout_kernel, gate_idx=2),
        out_shape=jax.ShapeDtypeStruct((b, n, d), F32),
        grid=(b, n // tm),
        in_specs=[pl.BlockSpec((1, tm, d), lambda i, j: (i, j, 0)),
                  pl.BlockSpec((1, tm, d), lambda i, j: (i, j, 0)),
                  pl.BlockSpec((1, tm, d), lambda i, j: (i, j, 0)),
                  pl.BlockSpec((1, tm, d), lambda i, j: (i, j, 1)),
                  pl.BlockSpec((d, d), lambda i, j: (0, 0)),
                  pl.BlockSpec((1, tm, d), lambda i, j: (i, j, 0)),
                  pl.BlockSpec((1, 6, d), lambda i, j: (i, 0, 0))],
        out_specs=pl.BlockSpec((1, tm, d), lambda i, j: (i, j, 0)),
        compiler_params=_params(("parallel", "parallel")),
        name="merge_outproj",
    )(ya, yb, p_gate, p_gate, w_o, x, mod3)


def _router_kernel(x_ref, w_ref, mod_ref, wr_ref, bias_ref, hs_ref, h16_ref, idx_ref, wgt_ref, *,
                   shift_idx, scale_idx):
    x = x_ref[0]
    y = x * lax.rsqrt(jnp.mean(x * x, axis=-1, keepdims=True) + EPS) * w_ref[...]
    m = mod_ref[0]
    h = y * (1.0 + m[scale_idx:scale_idx + 1]) + m[shift_idx:shift_idx + 1]
    tm, d = h.shape
    h_hi = h.astype(BF16)
    h16_ref[0] = h_hi
    packed = _pack_halves(h_hi)
    npk = d // 2 // V7X_LANES
    for c in range(npk):
        hs_ref[0, pl.ds(c, tm, stride=npk), :] = packed[:, c * V7X_LANES:(c + 1) * V7X_LANES]
    h_lo = (h - h_hi.astype(F32)).astype(BF16)
    wr = wr_ref[...]
    w_hi = wr.astype(BF16)
    w_lo = (wr - w_hi.astype(F32)).astype(BF16)
    logits = _dot_nt(w_hi, h_hi) + _dot_nt(w_hi, h_lo) + _dot_nt(w_lo, h_hi)
    scores = _sigmoid(logits)
    biased = scores + bias_ref[...]
    per_group = N_EXPERTS // N_GROUPS
    neg = -jnp.inf
    i32g = lax.broadcasted_iota(I32, (per_group, tm), 0).astype(F32)
    groups, gscore = [], []
    for g in range(N_GROUPS):
        sg = biased[g * per_group:(g + 1) * per_group, :]
        m1 = jnp.max(sg, axis=0, keepdims=True)
        a1 = jnp.min(jnp.where(sg == m1, i32g, float(per_group)), axis=0, keepdims=True)
        m2 = jnp.max(jnp.where(i32g == a1, neg, sg), axis=0, keepdims=True)
        groups.append(sg)
        gscore.append(m1 + m2)
    parts = []
    for g in range(N_GROUPS):
        ahead = jnp.zeros((1, tm), F32)
        for o in range(N_GROUPS):
            if o == g:
                continue
            beats = (gscore[o] >= gscore[g]) if o < g else (gscore[o] > gscore[g])
            ahead = ahead + jnp.where(beats, 1.0, 0.0)
        parts.append(jnp.where(ahead < float(TOPK_GROUPS), groups[g], neg))
    masked = jnp.concatenate(parts, axis=0)
    ei = lax.broadcasted_iota(I32, masked.shape, 0).astype(F32)
    idxs, wgts = [], []
    for _ in range(TOP_K):
        em = jnp.max(masked, axis=0, keepdims=True)
        ea = jnp.min(jnp.where(masked == em, ei, float(N_EXPERTS)), axis=0, keepdims=True)
        hit = ei == ea
        idxs.append(ea)
        wgts.append(jnp.sum(jnp.where(hit, scores, 0.0), axis=0, keepdims=True))
        masked = jnp.where(hit, neg, masked)
    idx = jnp.concatenate(idxs, axis=0)
    wgt = jnp.concatenate(wgts, axis=0)
    wgt = wgt / jnp.sum(wgt, axis=0, keepdims=True) * ROUTED_SCALE
    idx_ref[0] = idx.astype(I32)
    wgt_ref[0] = wgt


def _router(x1, norm_w, mod3, wr_t, bias_col):
    b, n, d = x1.shape
    tm = _pick(n, 256, V7X_LANES)
    nt = n // tm
    nl = d // 2 // V7X_LANES
    return pl.pallas_call(
        functools.partial(_router_kernel, shift_idx=3, scale_idx=4),
        out_shape=(jax.ShapeDtypeStruct((b, n * nl, V7X_LANES), jnp.uint32),
                   jax.ShapeDtypeStruct((b, n, d), BF16),
                   jax.ShapeDtypeStruct((b, TOP_K, n), I32),
                   jax.ShapeDtypeStruct((b, TOP_K, n), F32)),
        grid=(b, nt),
        in_specs=[pl.BlockSpec((1, tm, d), lambda i, j: (i, j, 0)),
                  pl.BlockSpec((1, d), lambda i, j: (0, 0)),
                  pl.BlockSpec((1, 6, d), lambda i, j: (i, 0, 0)),
                  pl.BlockSpec((N_EXPERTS, d), lambda i, j: (0, 0)),
                  pl.BlockSpec((N_EXPERTS, 1), lambda i, j: (0, 0))],
        out_specs=(pl.BlockSpec((1, tm * nl, V7X_LANES), lambda i, j: (i, j, 0)),
                   pl.BlockSpec((1, tm, d), lambda i, j: (i, j, 0)),
                   pl.BlockSpec((1, TOP_K, tm), lambda i, j: (i, 0, j)),
                   pl.BlockSpec((1, TOP_K, tm), lambda i, j: (i, 0, j))),
        compiler_params=_params(("parallel", "parallel")),
        name="ffn_norm_router",
    )(x1, norm_w, mod3, wr_t, bias_col)


def _spread(total, weights):
    acc, out, lo = 0.0, [], 0
    for w in weights:
        acc += w
        hi = round(total * acc / sum(weights))
        out.append(range(lo, hi))
        lo = hi
    return out


def _experts_kernel(be_ref, nu_ref, tok_ref, slot_ref, h_hbm, wg_ref, wu_ref, wd_ref, y_hbm,
                    xbuf, ybuf, gsem, ssem, *, bm):
    j = pl.program_id(0)
    n_used = nu_ref[0]
    nl = h_hbm.shape[1]
    d = 2 * nl * V7X_LANES
    de = wg_ref.shape[2]
    mxu = V7X_MXU_COLS
    pitch = EXPERT_SLAB_ROWS

    def wait_step_dmas():
        pltpu.make_async_copy(h_hbm.at[pl.ds(0, bm)], y_hbm.at[pl.ds(0, bm)], gsem).wait()
        pltpu.make_async_copy(h_hbm.at[pl.ds(0, bm)], y_hbm.at[pl.ds(0, bm)], ssem).wait()

    @pl.when(j == 0)
    def _():
        xbuf[...] = jnp.zeros(xbuf.shape, xbuf.dtype)
        ybuf[...] = jnp.zeros(ybuf.shape, ybuf.dtype)

    @pl.when((j >= 1) & (j <= n_used + 1))
    def _():
        wait_step_dmas()

    def step(nxt):
        cur = 1 - nxt

        def move_rows(rows):
            for r in rows:
                pltpu.make_async_copy(h_hbm.at[tok_ref[0, 0, r]],
                                      xbuf.at[nxt, pl.ds(r * pitch, nl)], gsem).start(priority=0)
                pltpu.make_async_copy(ybuf.at[nxt, pl.ds(r * pitch, nl)],
                                      y_hbm.at[slot_ref[0, 0, r]], ssem).start(priority=1)

        n_up, n_down = de // mxu, d // mxu
        lanes_per_slice = mxu // V7X_LANES
        parts = _spread(bm, [float(d)] * (2 * n_up) + [float(de)] * n_down)
        halves = [_unpack_halves(xbuf[cur, pl.ds(c, bm, stride=pitch), :]) for c in range(nl)]
        x = jnp.concatenate([lo for lo, _ in halves] + [hi for _, hi in halves], axis=1).astype(BF16)
        hid = []
        for s in range(n_up):
            cols = slice(s * mxu, (s + 1) * mxu)
            g = jnp.dot(x, wg_ref[0, :, cols].astype(BF16), preferred_element_type=F32)
            move_rows(parts[2 * s])
            u = jnp.dot(x, wu_ref[0, :, cols].astype(BF16), preferred_element_type=F32)
            move_rows(parts[2 * s + 1])
            hid.append((_silu(g) * u).astype(BF16))
        hid = jnp.concatenate(hid, axis=1)
        for s in range(n_down // 2):
            y_lo = jnp.dot(hid, wd_ref[0, :, s * mxu:(s + 1) * mxu].astype(BF16), preferred_element_type=F32)
            move_rows(parts[2 * n_up + 2 * s])
            y_hi = jnp.dot(hid, wd_ref[0, :, d // 2 + s * mxu:d // 2 + (s + 1) * mxu].astype(BF16),
                           preferred_element_type=F32)
            packed = _pack_halves(jnp.concatenate([y_lo, y_hi], axis=1))
            for c in range(lanes_per_slice):
                ybuf[cur, pl.ds(s * lanes_per_slice + c, bm, stride=pitch), :] = (
                    packed[:, c * V7X_LANES:(c + 1) * V7X_LANES])
            move_rows(parts[2 * n_up + 2 * s + 1])

    for parity in range(2):
        @pl.when((j <= n_used + 1) & (j % 2 == parity))
        def _(parity=parity):
            step(parity)

    @pl.when(j == n_used + 1)
    def _():
        wait_step_dmas()


def _experts(h2, block_expert, n_used, tok_buf, slot_buf, w_gate, w_up, w_down):
    t, nl, _ = h2.shape
    d = 2 * nl * V7X_LANES
    de = w_gate.shape[2]
    n_blocks, _, bm = tok_buf.shape
    assert de % V7X_MXU_COLS == 0 and d % (2 * V7X_MXU_COLS) == 0 and EXPERT_SLAB_ROWS >= nl
    last = n_blocks - 1

    def weights_of(j, be, nu):
        return (be[jnp.maximum(j - 1, 0)], 0, 0)

    grid_spec = pltpu.PrefetchScalarGridSpec(
        num_scalar_prefetch=2,
        grid=(n_blocks + 1,),
        in_specs=[
            pl.BlockSpec((1, 1, bm), lambda j, be, nu: (jnp.minimum(j, last), 0, 0), memory_space=pltpu.SMEM),
            pl.BlockSpec((1, 1, bm), lambda j, be, nu: (jnp.where(j >= 2, jnp.minimum(j - 2, last), last), 0, 0),
                         memory_space=pltpu.SMEM),
            pl.BlockSpec(memory_space=pl.ANY),
            pl.BlockSpec((1, d, de), weights_of),
            pl.BlockSpec((1, d, de), weights_of),
            pl.BlockSpec((1, de, d), weights_of),
        ],
        out_specs=pl.BlockSpec(memory_space=pl.ANY),
        scratch_shapes=[pltpu.VMEM((2, bm * EXPERT_SLAB_ROWS, V7X_LANES), jnp.uint32),
                        pltpu.VMEM((2, bm * EXPERT_SLAB_ROWS, V7X_LANES), jnp.uint32),
                        pltpu.SemaphoreType.DMA,
                        pltpu.SemaphoreType.DMA],
    )
    return pl.pallas_call(
        functools.partial(_experts_kernel, bm=bm),
        out_shape=jax.ShapeDtypeStruct((TOP_K * t + bm, nl, V7X_LANES), jnp.uint32),
        grid_spec=grid_spec,
        compiler_params=_params(("arbitrary",)),
        name="routed_experts",
    )(block_expert, n_used, tok_buf, slot_buf, h2, w_gate, w_up, w_down)


def _shared_kernel(h_ref, wg_ref, wu_ref, wd_ref, o_ref):
    x = h_ref[...]
    hid = _silu(jnp.dot(x, wg_ref[...], preferred_element_type=F32)) * jnp.dot(
        x, wu_ref[...], preferred_element_type=F32)
    o_ref[...] = jnp.dot(hid.astype(BF16), wd_ref[...], preferred_element_type=F32).astype(o_ref.dtype)


def _shared(h2, wg, wu, wd):
    t, d = h2.shape
    ds = wg.shape[1]
    tm = _pick(t, 512, 16)
    return pl.pallas_call(
        _shared_kernel,
        out_shape=jax.ShapeDtypeStruct((t, d), F32),
        grid=(t // tm,),
        in_specs=[pl.BlockSpec((tm, d), lambda i: (i, 0)),
                  pl.BlockSpec((d, ds), lambda i: (0, 0)),
                  pl.BlockSpec((d, ds), lambda i: (0, 0)),
                  pl.BlockSpec((ds, d), lambda i: (0, 0))],
        out_specs=pl.BlockSpec((tm, d), lambda i: (i, 0)),
        compiler_params=_params(("parallel",)),
        name="shared_expert",
    )(h2, wg, wu, wd)


def _combine_kernel(x_ref, wt_ref, sh_ref, mod_ref, fw_ref, *rest, gate_idx):
    yk_refs, o_ref = rest[:TOP_K], rest[TOP_K]
    wt = wt_ref[0]
    sh = sh_ref[0]
    tm, d = sh.shape
    nl = d // 2 // V7X_LANES
    lows, highs = [], []
    for c in range(nl):
        lo = sh[:, c * V7X_LANES:(c + 1) * V7X_LANES]
        hi = sh[:, d // 2 + c * V7X_LANES:d // 2 + (c + 1) * V7X_LANES]
        for kk in range(TOP_K):
            y_lo, y_hi = _unpack_halves(yk_refs[kk][pl.ds(c, tm, stride=nl), :])
            lo = lo + wt[:, kk:kk + 1] * y_lo
            hi = hi + wt[:, kk:kk + 1] * y_hi
        lows.append(lo)
        highs.append(hi)
    acc = jnp.concatenate(lows + highs, axis=1)
    m = mod_ref[0]
    x = x_ref[0] + m[gate_idx:gate_idx + 1] * acc
    o_ref[0] = x * lax.rsqrt(jnp.mean(x * x, axis=-1, keepdims=True) + EPS) * fw_ref[...]


def _combine(x1, yk, wtok, shared, mod3, final_w):
    b, n, d = x1.shape
    tm = _pick(n, 128, 16)
    nt = n // tm
    nl = d // 2 // V7X_LANES

    def yk_spec(kk):
        return pl.BlockSpec((tm * nl, V7X_LANES), lambda i, j: ((kk * b + i) * nt + j, 0))

    return pl.pallas_call(
        functools.partial(_combine_kernel, gate_idx=5),
        out_shape=jax.ShapeDtypeStruct((b, n, d), F32),
        grid=(b, nt),
        in_specs=[pl.BlockSpec((1, tm, d), lambda i, j: (i, j, 0)),
                  pl.BlockSpec((1, tm, TOP_K), lambda i, j: (i, j, 0)),
                  pl.BlockSpec((1, tm, d), lambda i, j: (i, j, 0)),
                  pl.BlockSpec((1, 6, d), lambda i, j: (i, 0, 0)),
                  pl.BlockSpec((1, d), lambda i, j: (0, 0))] + [yk_spec(kk) for kk in range(TOP_K)],
        out_specs=pl.BlockSpec((1, tm, d), lambda i, j: (i, j, 0)),
        compiler_params=_params(("parallel", "parallel")),
        name="combine_final_norm",
    )(x1, wtok, shared, mod3, final_w, *([yk] * TOP_K))


def _routing_tables(idx_kt, n_tokens):
    bm = EXPERT_BLOCK_ROWS
    n_assign = TOP_K * n_tokens
    assert n_assign % bm == 0
    n_blocks = n_assign // bm + N_EXPERTS
    e_flat = idx_kt.reshape(n_assign)
    experts = jnp.arange(N_EXPERTS, dtype=I32)
    counts = jnp.sum((e_flat[None, :] == experts[:, None]).astype(I32), axis=1)
    n_pad = (-counts) % bm
    big = 2 * N_EXPERTS
    fill = jnp.arange(bm, dtype=I32)[None, :] < n_pad[:, None]
    fill_keys = jnp.where(fill, 2 * experts[:, None] + 1, big).reshape(N_EXPERTS * bm)
    keys = jnp.concatenate([2 * e_flat, fill_keys])
    vals = jnp.concatenate([jnp.arange(n_assign, dtype=I32), jnp.full((N_EXPERTS * bm,), -1, I32)])
    keys, vals = lax.sort((keys, vals), num_keys=1)
    valid = vals >= 0
    tok = jnp.where(valid, vals % n_tokens, 0)
    slot = jnp.where(valid, vals, n_assign + jnp.arange(n_blocks * bm, dtype=I32) % bm)
    n_used = (jnp.sum(counts + n_pad) // bm).astype(I32)
    blk = jnp.minimum(jnp.arange(n_blocks, dtype=I32), n_used - 1)
    block_expert = keys.reshape(n_blocks, bm)[:, 0][blk] // 2
    return block_expert, n_used.reshape(1), tok.reshape(n_blocks, 1, bm), slot.reshape(n_blocks, 1, bm)


def kernel(x, c, ctx, c_ctx, w_ada, b_ada, norm_mix, norm_ffn, w_in, conv_a, conv_qkv, a_log, dt_bias,
           gdn_norm, w_proj_a, w_proj_b, w_out, w_router, router_bias, w_gate, w_up, w_down,
           ws_gate, ws_up, ws_down, final_norm):
    assert w_ada.shape[0] == 1, "single (final) layer only"
    b, seq, d = x.shape
    n_ctx = ctx.shape[1]
    t = b * seq
    dc = conv_a.shape[2]
    assert seq % GRID_W == 0 and seq % CHUNK == 0 and n_ctx % CHUNK == 0
    z_off = QKV_DIM + N_GATE_COLS
    conv_off = z_off + V_DIM
    gate_off = conv_off + 3 * dc
    assert w_in.shape[2] == gate_off + 2 * d

    rows = -(-(b + 1) // 8) * 8
    cpad = jnp.zeros((rows, d), F32).at[:b].set(c).at[b].set(c_ctx)
    mod3 = _ada(cpad, w_ada[0], b_ada).reshape(rows, 6, d)

    h = _norm_mod(x, norm_mix, mod3, lambda i: i, 0, 1, BF16).reshape(t, d)
    hc = _norm_mod(ctx, norm_mix, mod3, lambda i: b, 0, 1, BF16).reshape(b * n_ctx, d)
    w_in0 = w_in[0]
    w_qkv = w_in0[:, :QKV_DIM].astype(BF16)
    w_ab = w_in0[:, QKV_DIM:z_off].astype(BF16)
    p_qkv = _mm(h, w_qkv, BF16, "inproj_qkv").reshape(b, seq, QKV_DIM)
    p_ab = _mm(h, w_ab, F32, "inproj_ab").reshape(b, seq, N_GATE_COLS)
    p_z = _mm(h, w_in0[:, z_off:conv_off].astype(BF16), BF16, "inproj_z").reshape(b, seq, V_DIM)
    p_conv = _mm(h, w_in0[:, conv_off:gate_off].astype(BF16), BF16, "inproj_conv")
    p_gate = _mm(h, w_in0[:, gate_off:].astype(BF16), BF16, "inproj_gate").reshape(b, seq, 2 * d)
    pc_qkv = _mm(hc, w_qkv, BF16, "inproj_qkv_ctx").reshape(b, n_ctx, QKV_DIM)
    pc_ab = _mm(hc, w_ab, F32, "inproj_ab_ctx").reshape(b, n_ctx, N_GATE_COLS)

    zero = jnp.zeros((N_V_HEADS,), F32)
    prm = jnp.stack([jnp.concatenate([a_log[0, 0], zero, a_log[0, 1], zero]),
                     jnp.concatenate([dt_bias[0, 0], zero, dt_bias[0, 1], zero])])
    gcol, grow = _gates(p_ab, prm)
    gcol_c, grow_c = _gates(pc_ab, prm)
    yb_in = _gdn(p_qkv, p_z, gcol, grow, pc_qkv, gcol_c, grow_c, conv_qkv[0], gdn_norm)

    ya_in = _conva(p_conv, conv_a[0])
    ya = _mm(ya_in, w_proj_a[0].astype(BF16), BF16, "proj_a").reshape(b, seq, d)
    yb = _mm(yb_in.reshape(t, V_DIM), w_proj_b[0].astype(BF16), BF16, "proj_b").reshape(b, seq, d)
    x1 = _mixout(ya, yb, p_gate, w_out[0].astype(BF16), x, mod3)

    nl = d // 2 // V7X_LANES
    h2_slabs, h2_16, idx_bkt, wgt_bkt = _router(x1, norm_ffn, mod3, w_router[0].T,
                                                router_bias[0].reshape(N_EXPERTS, 1))
    idx_kt = jnp.transpose(idx_bkt, (1, 0, 2)).reshape(TOP_K, t)
    wtok = jnp.transpose(wgt_bkt, (0, 2, 1))
    block_expert, n_used, tok_buf, slot_buf = _routing_tables(idx_kt, t)
    yk = _experts(h2_slabs.reshape(t, nl, V7X_LANES), block_expert, n_used, tok_buf, slot_buf,
                  w_gate[0], w_up[0], w_down[0])
    shared = _shared(h2_16.reshape(t, d), ws_gate[0].astype(BF16), ws_up[0].astype(BF16),
                     ws_down[0].astype(BF16)).reshape(b, seq, d)
    return _combine(x1, yk.reshape(-1, V7X_LANES), wtok, shared, mod3, final_norm.reshape(1, d))
```

```python
import functools

import jax
import jax.numpy as jnp
from jax import lax
from jax.experimental import pallas as pl
from jax.experimental.pallas import tpu as pltpu

F32 = jnp.float32
BF16 = jnp.bfloat16
I32 = jnp.int32

EPS = 1e-6
GRID_W = 64
N_QK_HEADS = 16
N_V_HEADS = 32
HEAD_K = 128
HEAD_V = 128
CHUNK = 64
N_EXPERTS = 256
N_GROUPS = 8
TOPK_GROUPS = 4
TOP_K = 8
ROUTED_SCALE = 2.5

QK_DIM = N_QK_HEADS * HEAD_K
V_DIM = N_V_HEADS * HEAD_V
QKV_DIM = 2 * QK_DIM + V_DIM
N_GATE_COLS = 4 * N_V_HEADS

V7X_LANES = 128
V7X_VMEM_LIMIT_BYTES = 56 * 1024 * 1024
V7X_MXU_COLS = 256
EXPERT_BLOCK_ROWS = 256
EXPERT_ROW_GROUP = 8
EXPERT_SLAB_ROWS = 9
GDN_LOCAL_GROUP = 4


def _pick(n, target, mult):
    if n <= target:
        return n
    t = (target // mult) * mult
    while t >= mult:
        if n % t == 0:
            return t
        t -= mult
    raise ValueError(f"no tile for {n} (target {target}, multiple {mult})")


def _params(sem, vmem=V7X_VMEM_LIMIT_BYTES):
    return pltpu.CompilerParams(dimension_semantics=sem, vmem_limit_bytes=vmem)


def _sigmoid(x):
    return 1.0 / (1.0 + jnp.exp(-x))


def _silu(x):
    return x * _sigmoid(x)


def _dot(a, b):
    return jnp.dot(a.astype(BF16), b.astype(BF16), preferred_element_type=F32)


def _dot_nt(a, b):
    return lax.dot_general(a.astype(BF16), b.astype(BF16), (((1,), (1,)), ((), ())),
                           preferred_element_type=F32)


def _dot_tn(a, b):
    return lax.dot_general(a.astype(BF16), b.astype(BF16), (((0,), (0,)), ((), ())),
                           preferred_element_type=F32)


def _pack_halves(x):
    n = x.shape[1] // 2
    bits = lax.bitcast_convert_type(x.astype(BF16).astype(F32), jnp.uint32)
    return bits[:, n:] | lax.shift_right_logical(bits[:, :n], jnp.uint32(16))


def _unpack_halves(w):
    lo = lax.bitcast_convert_type(lax.shift_left(w, jnp.uint32(16)), F32)
    hi = lax.bitcast_convert_type(w & jnp.uint32(0xFFFF0000), F32)
    return lo, hi


def _split3(x):
    hi = x.astype(BF16)
    r = x - hi.astype(F32)
    mid = r.astype(BF16)
    lo = (r - mid.astype(F32)).astype(BF16)
    return hi, mid, lo


def _ada_kernel(c_ref, w_ref, b_ref, o_ref):
    a = _silu(c_ref[...])
    o_ref[...] = _dot(a, w_ref[...]) + b_ref[...]


def _ada(cpad, w, b):
    rows, d = cpad.shape
    n = w.shape[1]
    tn = _pick(n, 1024, V7X_LANES)
    return pl.pallas_call(
        _ada_kernel,
        out_shape=jax.ShapeDtypeStruct((rows, n), F32),
        grid=(n // tn,),
        in_specs=[pl.BlockSpec((rows, d), lambda j: (0, 0)),
                  pl.BlockSpec((d, tn), lambda j: (0, j)),
                  pl.BlockSpec((1, tn), lambda j: (0, j))],
        out_specs=pl.BlockSpec((rows, tn), lambda j: (0, j)),
        compiler_params=_params(("parallel",)),
        name="ada_mod",
    )(cpad, w, b)


def _norm_mod_kernel(x_ref, w_ref, mod_ref, o_ref, *, shift_idx, scale_idx):
    x = x_ref[0]
    y = x * lax.rsqrt(jnp.mean(x * x, axis=-1, keepdims=True) + EPS) * w_ref[...]
    m = mod_ref[0]
    o_ref[0] = (y * (1.0 + m[scale_idx:scale_idx + 1]) + m[shift_idx:shift_idx + 1]).astype(o_ref.dtype)


def _norm_mod(x, w, mod3, row_of_batch, shift_idx, scale_idx, out_dtype):
    b, n, d = x.shape
    ts = _pick(n, 512, 16)
    return pl.pallas_call(
        functools.partial(_norm_mod_kernel, shift_idx=shift_idx, scale_idx=scale_idx),
        out_shape=jax.ShapeDtypeStruct((b, n, d), out_dtype),
        grid=(b, n // ts),
        in_specs=[pl.BlockSpec((1, ts, d), lambda i, j: (i, j, 0)),
                  pl.BlockSpec((1, d), lambda i, j: (0, 0)),
                  pl.BlockSpec((1, 6, d), lambda i, j: (row_of_batch(i), 0, 0))],
        out_specs=pl.BlockSpec((1, ts, d), lambda i, j: (i, j, 0)),
        compiler_params=_params(("parallel", "parallel")),
        name="norm_mod",
    )(x, w, mod3)


def _mm_kernel(a_ref, w_ref, o_ref):
    o_ref[...] = jnp.dot(a_ref[...], w_ref[...], preferred_element_type=F32).astype(o_ref.dtype)


def _mm(a, w, out_dtype, name):
    m, k = a.shape
    n = w.shape[1]
    tm = _pick(m, 1024, 16)
    tn = _pick(n, 1024, V7X_LANES)
    return pl.pallas_call(
        _mm_kernel,
        out_shape=jax.ShapeDtypeStruct((m, n), out_dtype),
        grid=(n // tn, m // tm),
        in_specs=[pl.BlockSpec((tm, k), lambda j, i: (i, 0)),
                  pl.BlockSpec((k, tn), lambda j, i: (0, j))],
        out_specs=pl.BlockSpec((tm, tn), lambda j, i: (i, j)),
        compiler_params=_params(("parallel", "parallel")),
        name=name,
    )(a, w)


def _gates_kernel(ab_ref, prm_ref, col_ref, row_ref, *, rows):
    sub = 2 * CHUNK
    n_sub = rows // sub
    prm = prm_ref[...]
    alog = prm[0:1]
    dtb = prm[1:2]
    lane = lax.broadcasted_iota(I32, (sub, N_GATE_COLS), 1)
    is_g = (lane // N_V_HEADS) % 2 == 0
    is_bwd = lane >= 2 * N_V_HEADS
    r = lax.broadcasted_iota(I32, (sub, sub), 0)
    c = lax.broadcasted_iota(I32, (sub, sub), 1)
    same = (r // CHUNK) == (c // CHUNK)
    cum_f = jnp.where(same & (c <= r), 1.0, 0.0).astype(BF16)
    cum_b = jnp.where(same & (c >= r), 1.0, 0.0).astype(BF16)
    for s in range(n_sub):
        x = ab_ref[0, s * sub:(s + 1) * sub, :]
        xs = x + dtb
        softplus = jnp.maximum(xs, 0.0) + jnp.log(1.0 + jnp.exp(-jnp.abs(xs)))
        g = -jnp.exp(alog) * softplus
        beta = _sigmoid(x)
        hi, mid, lo = _split3(g)
        pf = (jnp.dot(cum_f, hi, preferred_element_type=F32) + jnp.dot(cum_f, mid, preferred_element_type=F32)
              + jnp.dot(cum_f, lo, preferred_element_type=F32))
        pb = (jnp.dot(cum_b, hi, preferred_element_type=F32) + jnp.dot(cum_b, mid, preferred_element_type=F32)
              + jnp.dot(cum_b, lo, preferred_element_type=F32))
        out = jnp.where(is_g, jnp.where(is_bwd, pb, pf), beta)
        col_ref[0, s * sub:(s + 1) * sub, :] = out
        for h in range(2):
            oc = out[h * CHUNK:(h + 1) * CHUNK]
            row_ref[0, 2 * s + h] = jnp.concatenate([oc, oc], axis=0).T


def _gates(ab, prm):
    b, n, g = ab.shape
    rows = _pick(n, 512, 2 * CHUNK)
    return pl.pallas_call(
        functools.partial(_gates_kernel, rows=rows),
        out_shape=(jax.ShapeDtypeStruct((b, n, g), F32),
                   jax.ShapeDtypeStruct((b, n // CHUNK, g, 2 * CHUNK), F32)),
        grid=(b, n // rows),
        in_specs=[pl.BlockSpec((1, rows, g), lambda i, j: (i, j, 0)),
                  pl.BlockSpec((2, g), lambda i, j: (0, 0))],
        out_specs=(pl.BlockSpec((1, rows, g), lambda i, j: (i, j, 0)),
                   pl.BlockSpec((1, rows // CHUNK, g, 2 * CHUNK), lambda i, j: (i, j, 0, 0))),
        compiler_params=_params(("parallel", "parallel")),
        name="gate_prep",
    )(ab, prm)


def _conv_silu_tile(src_ref, w, base, rows, n_total):
    x = src_ref[0, pl.ds(base, rows), :].astype(F32)
    pb = pl.multiple_of(jnp.maximum(base - 16, 0), 16)
    nb = pl.multiple_of(jnp.minimum(base + rows, n_total - 16), 16)
    prev = src_ref[0, pl.ds(pb, 16), :].astype(F32)[15:16]
    nxt = src_ref[0, pl.ds(nb, 16), :].astype(F32)[0:1]
    prev = jnp.where(base > 0, prev, 0.0)
    nxt = jnp.where(base + rows < n_total, nxt, 0.0)
    ridx = lax.broadcasted_iota(I32, x.shape, 0)
    xp = jnp.where(ridx == 0, prev, pltpu.roll(x, 1, 0))
    xn = jnp.where(ridx == rows - 1, nxt, pltpu.roll(x, rows - 1, 0))
    y = xp * w[0:1] + x * w[1:2] + xn * w[2:3]
    return _silu(y)


def _l2norm_rows(t):
    return t * lax.rsqrt(jnp.sum(t * t, axis=-1, keepdims=True) + EPS)


def _gdn_kernel(q_ref, k_ref, v_ref, z_ref, gc_ref, gt_ref,
                qc_ref, kc_ref, vc_ref, gcc_ref, gtc_ref,
                cq_ref, ck_ref, cv_ref, gn_ref,
                y_ref,
                qs, ks, vs, osum, st, u_s, w_s, *, n_lat, n_ctx):
    hq = pl.program_id(1)
    rep = N_V_HEADS // N_QK_HEADS
    wq = cq_ref[...]
    wk = ck_ref[...]
    wv = cv_ref[...]

    def prep(src_q, src_k, src_v, n_total, row0):
        tile = _pick(n_total, 256, 16)

        def body(t, carry):
            base = pl.multiple_of(t * tile, tile)
            dst = pl.multiple_of(row0 + base, 16)
            qs[pl.ds(dst, tile), :] = _l2norm_rows(_conv_silu_tile(src_q, wq, base, tile, n_total)) * (HEAD_K ** -0.5)
            ks[pl.ds(dst, tile), :] = _l2norm_rows(_conv_silu_tile(src_k, wk, base, tile, n_total))
            vs[pl.ds(dst, tile), :] = _conv_silu_tile(src_v, wv, base, tile, n_total)
            return carry

        lax.fori_loop(0, n_total // tile, body, 0)

    prep(qc_ref, kc_ref, vc_ref, n_ctx, 0)
    prep(q_ref, k_ref, v_ref, n_lat, n_ctx)

    st[...] = jnp.zeros(st.shape, F32)
    osum[...] = jnp.zeros(osum.shape, F32)

    wide = (CHUNK, 2 * CHUNK)
    ri = lax.broadcasted_iota(I32, wide, 0)
    li = lax.broadcasted_iota(I32, wide, 1)
    ci = li % CHUNK
    left = li < CHUNK
    eye_left = jnp.where(left & (ri == ci), 1.0, 0.0)
    incl = (ri >= ci, ri <= ci)
    strict = (ri > ci, ri < ci)
    last = (CHUNK - 1, 0)
    lane = lax.broadcasted_iota(I32, (CHUNK, N_GATE_COLS), 1)
    zeros_sq = jnp.zeros(wide, BF16)
    zeros_uw = jnp.zeros((CHUNK, 2 * HEAD_V), BF16)

    def gate_col(gtile, col):
        return jnp.sum(jnp.where(lane == col, gtile, 0.0), axis=1, keepdims=True)

    def decay_of(gcol, grow2, dirn):
        return jnp.where(incl[dirn], jnp.exp(jnp.where(incl[dirn], gcol - grow2, 0.0)), 0.0)

    def mm(a, b):
        return jnp.dot(a, b, preferred_element_type=F32)

    grp = GDN_LOCAL_GROUP

    def chunk_of(seq, g, j, dirn):
        n_chunks = seq[3]
        return grp * g + j if dirn == 0 else n_chunks - 1 - grp * g - j

    def local_stages(seq, g, slot):
        gcol_ref, grow_ref, row0, _ = seq
        box = {}

        def prep():
            sbs, rhss, dests = [], [], []
            for j in range(grp):
                for dirn in range(2):
                    c = chunk_of(seq, g, j, dirn)
                    r = pl.multiple_of(row0 + c * CHUNK, CHUNK)
                    k = ks[pl.ds(r, CHUNK), :]
                    vv = vs[pl.ds(r, CHUNK), :]
                    k16 = k.astype(BF16)
                    kk2 = _dot_nt(k16, jnp.concatenate([k16, k16], axis=0))
                    gtile = gcol_ref[0, pl.ds(pl.multiple_of(c * CHUNK, CHUNK), CHUNK), :]
                    for head in range(rep):
                        col_g = (2 * dirn) * N_V_HEADS + hq * rep + head
                        gcol = gate_col(gtile, col_g)
                        bcol = gate_col(gtile, col_g + N_V_HEADS)
                        grow2 = grow_ref[0, c, 2 * dirn, pl.ds(head, 1), :]
                        a2 = bcol * kk2 * decay_of(gcol, grow2, dirn)
                        sbs.append(jnp.where(left, eye_left, jnp.where(strict[dirn], -a2, 0.0)))
                        v = vv[:, head * HEAD_V:(head + 1) * HEAD_V]
                        rhss.append(jnp.concatenate([v * bcol, k * (bcol * jnp.exp(gcol))], axis=1).astype(BF16))
                        dests.append((dirn * rep + head, j))
            box.update(sbs=sbs, rhss=rhss, dests=dests)

        def double():
            nxt = []
            for sb in box["sbs"]:
                sb16 = sb.astype(BF16)
                nxt.append(mm(sb16, jnp.concatenate([zeros_sq, sb16], axis=0)) + jnp.where(left, sb, 0.0))
            box["sbs"] = nxt

        def apply_s32():
            box["sb16s"] = [sb.astype(BF16) for sb in box["sbs"]]
            box["xs"] = [mm(sb16, jnp.concatenate([rhs, zeros_uw], axis=0))
                         for sb16, rhs in zip(box["sb16s"], box["rhss"])]

        def finish():
            for sb16, x, (chain, j) in zip(box["sb16s"], box["xs"], box["dests"]):
                uw = x + mm(sb16, jnp.concatenate([zeros_uw, x.astype(BF16)], axis=0))
                u_s[slot, chain, j * CHUNK:(j + 1) * CHUNK, :] = uw[:, :HEAD_V]
                w_s[slot, chain, j * CHUNK:(j + 1) * CHUNK, :] = uw[:, HEAD_V:].astype(BF16)

        return [prep] + [double] * 5 + [apply_s32, finish]

    def scan_stages(seq, g, j, slot, write_out):
        gcol_ref, grow_ref, row0, _ = seq
        box = {}

        def inputs():
            work = []
            for dirn in range(2):
                c = chunk_of(seq, g, j, dirn)
                r = pl.multiple_of(row0 + c * CHUNK, CHUNK)
                lr = pl.multiple_of(c * CHUNK, CHUNK)
                k = ks[pl.ds(r, CHUNK), :]
                gtile = gcol_ref[0, pl.ds(lr, CHUNK), :]
                if write_out:
                    q = qs[pl.ds(r, CHUNK), :]
                    k16 = k.astype(BF16)
                    qk2 = _dot_nt(q, jnp.concatenate([k16, k16], axis=0))
                for head in range(rep):
                    chain = dirn * rep + head
                    gcol = gate_col(gtile, (2 * dirn) * N_V_HEADS + hq * rep + head)
                    glast = gcol[last[dirn]:last[dirn] + 1, :]
                    kdec16 = (k * jnp.exp(glast - gcol)).astype(BF16)
                    lhs = w_s[slot, chain, j * CHUNK:(j + 1) * CHUNK, :]
                    p16 = None
                    if write_out:
                        grow2 = grow_ref[0, c, 2 * dirn, pl.ds(head, 1), :]
                        p16 = (qk2 * decay_of(gcol, grow2, dirn)).astype(BF16)
                        lhs = jnp.concatenate([lhs, (q * jnp.exp(gcol)).astype(BF16)], axis=0)
                    work.append((chain, lr, head, glast, kdec16, lhs, p16))
            box["work"] = work

        def state_matmul():
            box["s_olds"] = [st[wk[0]] for wk in box["work"]]
            box["tbs"] = [mm(wk[5], s_old.astype(BF16))
                          for wk, s_old in zip(box["work"], box["s_olds"])]

        def update():
            for (chain, lr, head, glast, kdec16, _, p16), s_old, tb in zip(box["work"], box["s_olds"], box["tbs"]):
                vn16 = (u_s[slot, chain, j * CHUNK:(j + 1) * CHUNK, :] - tb[:CHUNK]).astype(BF16)
                if write_out:
                    o = tb[CHUNK:] + mm(p16, jnp.concatenate([vn16, zeros_sq], axis=0))
                    osum[pl.ds(lr, CHUNK), head * HEAD_V:(head + 1) * HEAD_V] += o
                st[chain] = s_old * jnp.exp(glast) + _dot_tn(kdec16, vn16)

        return [inputs, state_matmul, update]

    def run_group(scan_of, local_of):
        local = local_stages(*local_of) if local_of else []
        steps = [scan_stages(scan_of[0], scan_of[1], j, scan_of[2], scan_of[3]) for j in range(grp)] if scan_of else []
        pending = list(local)
        for step in steps:
            step[0]()
            for stage in step[1:]:
                if pending:
                    pending.pop(0)()
                stage()
        for stage in pending:
            stage()

    ctx_seq = (gcc_ref, gtc_ref, 0, n_ctx // CHUNK)
    lat_seq = (gc_ref, gt_ref, n_ctx, n_lat // CHUNK)
    ng_ctx, ng_lat = n_ctx // CHUNK // grp, n_lat // CHUNK // grp
    run_group(None, (ctx_seq, 0, 0))
    for g in range(ng_ctx - 1):
        run_group((ctx_seq, g, g % 2, False), (ctx_seq, g + 1, (g + 1) % 2))
    run_group((ctx_seq, ng_ctx - 1, (ng_ctx - 1) % 2, False), (lat_seq, 0, ng_ctx % 2))

    def lat_body(g, carry):
        run_group((lat_seq, g, (ng_ctx + g) % 2, True), (lat_seq, g + 1, (ng_ctx + g + 1) % 2))
        return carry

    lax.fori_loop(0, ng_lat - 1, lat_body, 0)
    run_group((lat_seq, ng_lat - 1, (ng_ctx + ng_lat - 1) % 2, True), None)

    gn = gn_ref[...]
    tile = _pick(n_lat, 256, 16)

    def epi(t, carry):
        base = pl.multiple_of(t * tile, tile)
        o = osum[pl.ds(base, tile), :]
        z = z_ref[0, pl.ds(base, tile), :].astype(F32)
        for head in range(rep):
            sl = slice(head * HEAD_V, (head + 1) * HEAD_V)
            oh = o[:, sl]
            y = oh * lax.rsqrt(jnp.mean(oh * oh, axis=-1, keepdims=True) + EPS) * gn * _silu(z[:, sl])
            y_ref[0, pl.ds(base, tile), sl] = y.astype(y_ref.dtype)
        return carry

    lax.fori_loop(0, n_lat // tile, epi, 0)


def _gdn(p_qkv, p_z, gcol, grow, pc_qkv, gcol_c, grow_c, conv_qkv, gdn_norm):
    b, n_lat, _ = p_qkv.shape
    n_ctx = pc_qkv.shape[1]
    rep = N_V_HEADS // N_QK_HEADS
    vw = rep * HEAD_V
    kq0 = QK_DIM // HEAD_K
    v0 = 2 * QK_DIM // vw
    nc_lat, nc_ctx = n_lat // CHUNK, n_ctx // CHUNK
    assert nc_lat % GDN_LOCAL_GROUP == 0 and nc_ctx % GDN_LOCAL_GROUP == 0
    assert HEAD_K == HEAD_V == N_GATE_COLS == 2 * CHUNK
    grow5 = grow.reshape(b, nc_lat, 4, N_QK_HEADS, rep, 2 * CHUNK)
    grow5_c = grow_c.reshape(b, nc_ctx, 4, N_QK_HEADS, rep, 2 * CHUNK)

    def seq_specs(n):
        return [pl.BlockSpec((1, n, HEAD_K), lambda i, h: (i, 0, h)),
                pl.BlockSpec((1, n, HEAD_K), lambda i, h: (i, 0, kq0 + h)),
                pl.BlockSpec((1, n, vw), lambda i, h: (i, 0, v0 + h))]

    def gate_specs(n, nc):
        return [pl.BlockSpec((1, n, N_GATE_COLS), lambda i, h: (i, 0, 0)),
                pl.BlockSpec((1, nc, 4, None, rep, 2 * CHUNK), lambda i, h: (i, 0, 0, h, 0, 0))]

    in_specs = (seq_specs(n_lat)
                + [pl.BlockSpec((1, n_lat, vw), lambda i, h: (i, 0, h))]
                + gate_specs(n_lat, nc_lat)
                + seq_specs(n_ctx)
                + gate_specs(n_ctx, nc_ctx)
                + [pl.BlockSpec((3, HEAD_K), lambda i, h: (0, h)),
                   pl.BlockSpec((3, HEAD_K), lambda i, h: (0, kq0 + h)),
                   pl.BlockSpec((3, vw), lambda i, h: (0, v0 + h)),
                   pl.BlockSpec((1, HEAD_V), lambda i, h: (0, 0))])
    n_all = n_ctx + n_lat
    return pl.pallas_call(
        functools.partial(_gdn_kernel, n_lat=n_lat, n_ctx=n_ctx),
        out_shape=jax.ShapeDtypeStruct((b, n_lat, V_DIM), BF16),
        grid=(b, N_QK_HEADS),
        in_specs=in_specs,
        out_specs=pl.BlockSpec((1, n_lat, vw), lambda i, h: (i, 0, h)),
        scratch_shapes=[pltpu.VMEM((n_all, HEAD_K), F32),
                        pltpu.VMEM((n_all, HEAD_K), F32),
                        pltpu.VMEM((n_all, vw), F32),
                        pltpu.VMEM((n_lat, vw), F32),
                        pltpu.VMEM((2 * rep, HEAD_K, HEAD_V), F32),
                        pltpu.VMEM((2, 2 * rep, GDN_LOCAL_GROUP * CHUNK, HEAD_V), F32),
                        pltpu.VMEM((2, 2 * rep, GDN_LOCAL_GROUP * CHUNK, HEAD_K), BF16)],
        compiler_params=_params(("parallel", "parallel")),
        name="gdn_bidir",
    )(p_qkv, p_qkv, p_qkv, p_z, gcol, grow5,
      pc_qkv, pc_qkv, pc_qkv, gcol_c, grow5_c,
      conv_qkv, conv_qkv, conv_qkv, gdn_norm)


def _conva_kernel(gb_ref, gcx_ref, xin_ref, w_ref, o_ref):
    u = gcx_ref[...].astype(F32) * xin_ref[...].astype(F32)
    rows = u.shape[0]
    pos = lax.broadcasted_iota(I32, u.shape, 0) % GRID_W
    up = jnp.where(pos == 0, 0.0, pltpu.roll(u, 1, 0))
    un = jnp.where(pos == GRID_W - 1, 0.0, pltpu.roll(u, rows - 1, 0))
    w = w_ref[...]
    conv = up * w[0:1] + u * w[1:2] + un * w[2:3]
    o_ref[...] = (gb_ref[...].astype(F32) * conv).astype(o_ref.dtype)


def _conva(p_conv, conv_w):
    t, c3 = p_conv.shape
    dc = c3 // 3
    tr = _pick(t, 512, GRID_W)
    tc = _pick(dc, 512, V7X_LANES)
    nj = dc // tc
    return pl.pallas_call(
        _conva_kernel,
        out_shape=jax.ShapeDtypeStruct((t, dc), BF16),
        grid=(t // tr, nj),
        in_specs=[pl.BlockSpec((tr, tc), lambda i, j: (i, j)),
                  pl.BlockSpec((tr, tc), lambda i, j: (i, nj + j)),
                  pl.BlockSpec((tr, tc), lambda i, j: (i, 2 * nj + j)),
                  pl.BlockSpec((3, tc), lambda i, j: (0, j))],
        out_specs=pl.BlockSpec((tr, tc), lambda i, j: (i, j)),
        compiler_params=_params(("parallel", "parallel")),
        name="mixer_a_conv",
    )(p_conv, p_conv, p_conv, conv_w)


def _mixout_kernel(ya_ref, yb_ref, ga_ref, gb_ref, wo_ref, x_ref, mod_ref, o_ref, *, gate_idx):
    mix = (_sigmoid(ga_ref[0].astype(F32)) * ya_ref[0].astype(F32)
           + _sigmoid(gb_ref[0].astype(F32)) * yb_ref[0].astype(F32))
    out = _dot(mix, wo_ref[...])
    m = mod_ref[0]
    o_ref[0] = x_ref[0] + m[gate_idx:gate_idx + 1] * out


def _mixout(ya, yb, p_gate, w_o, x, mod3):
    b, n, d = x.shape
    tm = _pick(n, 512, 16)
    return pl.pallas_call(
        functools.partial(_mixout_kernel, gate_idx=2),
        out_shape=jax.ShapeDtypeStruct((b, n, d), F32),
        grid=(b, n // tm),
        in_specs=[pl.BlockSpec((1, tm, d), lambda i, j: (i, j, 0)),
                  pl.BlockSpec((1, tm, d), lambda i, j: (i, j, 0)),
                  pl.BlockSpec((1, tm, d), lambda i, j: (i, j, 0)),
                  pl.BlockSpec((1, tm, d), lambda i, j: (i, j, 1)),
                  pl.BlockSpec((d, d), lambda i, j: (0, 0)),
                  pl.BlockSpec((1, tm, d), lambda i, j: (i, j, 0)),
                  pl.BlockSpec((1, 6, d), lambda i, j: (i, 0, 0))],
        out_specs=pl.BlockSpec((1, tm, d), lambda i, j: (i, j, 0)),
        compiler_params=_params(("parallel", "parallel")),
        name="merge_outproj",
    )(ya, yb, p_gate, p_gate, w_o, x, mod3)


def _router_kernel(x_ref, w_ref, mod_ref, wr_ref, bias_ref, hs_ref, h16_ref, idx_ref, wgt_ref, *,
                   shift_idx, scale_idx):
    x = x_ref[0]
    y = x * lax.rsqrt(jnp.mean(x * x, axis=-1, keepdims=True) + EPS) * w_ref[...]
    m = mod_ref[0]
    h = y * (1.0 + m[scale_idx:scale_idx + 1]) + m[shift_idx:shift_idx + 1]
    tm, d = h.shape
    h_hi = h.astype(BF16)
    h16_ref[0] = h_hi
    packed = _pack_halves(h_hi)
    npk = d // 2 // V7X_LANES
    for c in range(npk):
        hs_ref[0, pl.ds(c, tm, stride=npk), :] = packed[:, c * V7X_LANES:(c + 1) * V7X_LANES]
    h_lo = (h - h_hi.astype(F32)).astype(BF16)
    wr = wr_ref[...]
    w_hi = wr.astype(BF16)
    w_lo = (wr - w_hi.astype(F32)).astype(BF16)
    logits = _dot_nt(w_hi, h_hi) + _dot_nt(w_hi, h_lo) + _dot_nt(w_lo, h_hi)
    scores = _sigmoid(logits)
    biased = scores + bias_ref[...]
    per_group = N_EXPERTS // N_GROUPS
    neg = -jnp.inf
    i32g = lax.broadcasted_iota(I32, (per_group, tm), 0).astype(F32)
    groups, gscore = [], []
    for g in range(N_GROUPS):
        sg = biased[g * per_group:(g + 1) * per_group, :]
        m1 = jnp.max(sg, axis=0, keepdims=True)
        a1 = jnp.min(jnp.where(sg == m1, i32g, float(per_group)), axis=0, keepdims=True)
        m2 = jnp.max(jnp.where(i32g == a1, neg, sg), axis=0, keepdims=True)
        groups.append(sg)
        gscore.append(m1 + m2)
    parts = []
    for g in range(N_GROUPS):
        ahead = jnp.zeros((1, tm), F32)
        for o in range(N_GROUPS):
            if o == g:
                continue
            beats = (gscore[o] >= gscore[g]) if o < g else (gscore[o] > gscore[g])
            ahead = ahead + jnp.where(beats, 1.0, 0.0)
        parts.append(jnp.where(ahead < float(TOPK_GROUPS), groups[g], neg))
    masked = jnp.concatenate(parts, axis=0)
    ei = lax.broadcasted_iota(I32, masked.shape, 0).astype(F32)
    idxs, wgts = [], []
    for _ in range(TOP_K):
        em = jnp.max(masked, axis=0, keepdims=True)
        ea = jnp.min(jnp.where(masked == em, ei, float(N_EXPERTS)), axis=0, keepdims=True)
        hit = ei == ea
        idxs.append(ea)
        wgts.append(jnp.sum(jnp.where(hit, scores, 0.0), axis=0, keepdims=True))
        masked = jnp.where(hit, neg, masked)
    idx = jnp.concatenate(idxs, axis=0)
    wgt = jnp.concatenate(wgts, axis=0)
    wgt = wgt / jnp.sum(wgt, axis=0, keepdims=True) * ROUTED_SCALE
    idx_ref[0] = idx.astype(I32)
    wgt_ref[0] = wgt


def _router(x1, norm_w, mod3, wr_t, bias_col):
    b, n, d = x1.shape
    tm = _pick(n, 256, V7X_LANES)
    nt = n // tm
    nl = d // 2 // V7X_LANES
    return pl.pallas_call(
        functools.partial(_router_kernel, shift_idx=3, scale_idx=4),
        out_shape=(jax.ShapeDtypeStruct((b, n * nl, V7X_LANES), jnp.uint32),
                   jax.ShapeDtypeStruct((b, n, d), BF16),
                   jax.ShapeDtypeStruct((b, TOP_K, n), I32),
                   jax.ShapeDtypeStruct((b, TOP_K, n), F32)),
        grid=(b, nt),
        in_specs=[pl.BlockSpec((1, tm, d), lambda i, j: (i, j, 0)),
                  pl.BlockSpec((1, d), lambda i, j: (0, 0)),
                  pl.BlockSpec((1, 6, d), lambda i, j: (i, 0, 0)),
                  pl.BlockSpec((N_EXPERTS, d), lambda i, j: (0, 0)),
                  pl.BlockSpec((N_EXPERTS, 1), lambda i, j: (0, 0))],
        out_specs=(pl.BlockSpec((1, tm * nl, V7X_LANES), lambda i, j: (i, j, 0)),
                   pl.BlockSpec((1, tm, d), lambda i, j: (i, j, 0)),
                   pl.BlockSpec((1, TOP_K, tm), lambda i, j: (i, 0, j)),
                   pl.BlockSpec((1, TOP_K, tm), lambda i, j: (i, 0, j))),
        compiler_params=_params(("parallel", "parallel")),
        name="ffn_norm_router",
    )(x1, norm_w, mod3, wr_t, bias_col)


def _spread(total, weights):
    acc, out, lo = 0.0, [], 0
    for w in weights:
        acc += w
        hi = round(total * acc / sum(weights))
        out.append(range(lo, hi))
        lo = hi
    return out


def _experts_kernel(be_ref, nv_ref, nu_ref, tok_ref, slot_ref, h_hbm, wg_ref, wu_ref, wd_ref, y_hbm,
                    xbuf, ybuf, gsem, ssem, *, bm):
    j = pl.program_id(0)
    n_used = nu_ref[0]
    nl = h_hbm.shape[1]
    d = 2 * nl * V7X_LANES
    de = wg_ref.shape[2]
    mxu = V7X_MXU_COLS
    pitch = EXPERT_SLAB_ROWS

    rg = EXPERT_ROW_GROUP
    n_blocks = nv_ref.shape[0]

    def rows_up(n):
        return (n + (rg - 1)) & ~(rg - 1)

    def gather_rows(s):
        return rows_up(jnp.where(s < n_used, nv_ref[jnp.clip(s, 0, n_blocks - 1)], 0))

    def scatter_rows(s):
        real = jnp.where((s >= 2) & (s - 2 < n_used), nv_ref[jnp.clip(s - 2, 0, n_blocks - 1)], 0)
        return rows_up(jnp.where(s == 0, bm, real))

    def wait_step_dmas(s):
        for n, sem in ((gather_rows(s), gsem), (scatter_rows(s), ssem)):
            @pl.when(n > 0)
            def _(n=n, sem=sem):
                pltpu.make_async_copy(h_hbm.at[pl.ds(0, n)], y_hbm.at[pl.ds(0, n)], sem).wait()

    @pl.when(j == 0)
    def _():
        xbuf[...] = jnp.zeros(xbuf.shape, xbuf.dtype)
        ybuf[...] = jnp.zeros(ybuf.shape, ybuf.dtype)

    @pl.when((j >= 1) & (j <= n_used + 1))
    def _():
        wait_step_dmas(j - 1)

    def step(nxt):
        cur = 1 - nxt
        n_gather, n_scatter = gather_rows(j), scatter_rows(j)

        def move_rows(groups):
            for g in groups:
                @pl.when(g * rg < n_gather)
                def _(g=g):
                    for r in range(g * rg, (g + 1) * rg):
                        pltpu.make_async_copy(h_hbm.at[tok_ref[0, 0, r]],
                                              xbuf.at[nxt, pl.ds(r * pitch, nl)], gsem).start()

                @pl.when(g * rg < n_scatter)
                def _(g=g):
                    for r in range(g * rg, (g + 1) * rg):
                        pltpu.make_async_copy(ybuf.at[nxt, pl.ds(r * pitch, nl)],
                                              y_hbm.at[slot_ref[0, 0, r]], ssem).start()

        n_up, n_down = de // mxu, d // mxu
        lanes_per_slice = mxu // V7X_LANES
        parts = _spread(bm // rg, [float(d)] * (2 * n_up) + [float(de)] * n_down)
        halves = [_unpack_halves(xbuf[cur, pl.ds(c, bm, stride=pitch), :]) for c in range(nl)]
        x = jnp.concatenate([lo for lo, _ in halves] + [hi for _, hi in halves], axis=1).astype(BF16)
        hid = []
        for s in range(n_up):
            cols = slice(s * mxu, (s + 1) * mxu)
            g = jnp.dot(x, wg_ref[0, :, cols].astype(BF16), preferred_element_type=F32)
            move_rows(parts[2 * s])
            u = jnp.dot(x, wu_ref[0, :, cols].astype(BF16), preferred_element_type=F32)
            move_rows(parts[2 * s + 1])
            hid.append((_silu(g) * u).astype(BF16))
        hid = jnp.concatenate(hid, axis=1)
        for s in range(n_down // 2):
            y_lo = jnp.dot(hid, wd_ref[0, :, s * mxu:(s + 1) * mxu].astype(BF16), preferred_element_type=F32)
            move_rows(parts[2 * n_up + 2 * s])
            y_hi = jnp.dot(hid, wd_ref[0, :, d // 2 + s * mxu:d // 2 + (s + 1) * mxu].astype(BF16),
                           preferred_element_type=F32)
            packed = _pack_halves(jnp.concatenate([y_lo, y_hi], axis=1))
            for c in range(lanes_per_slice):
                ybuf[cur, pl.ds(s * lanes_per_slice + c, bm, stride=pitch), :] = (
                    packed[:, c * V7X_LANES:(c + 1) * V7X_LANES])
            move_rows(parts[2 * n_up + 2 * s + 1])

    for parity in range(2):
        @pl.when((j <= n_used + 1) & (j % 2 == parity))
        def _(parity=parity):
            step(parity)

    @pl.when(j == n_used + 1)
    def _():
        wait_step_dmas(j)


def _experts(h2, block_expert, n_valid, n_used, tok_buf, slot_buf, w_gate, w_up, w_down):
    t, nl, _ = h2.shape
    d = 2 * nl * V7X_LANES
    de = w_gate.shape[2]
    n_blocks, _, bm = tok_buf.shape
    assert de % V7X_MXU_COLS == 0 and d % (2 * V7X_MXU_COLS) == 0 and EXPERT_SLAB_ROWS >= nl
    last = n_blocks - 1

    def weights_of(j, be, nv, nu):
        return (be[jnp.maximum(j - 1, 0)], 0, 0)

    assert bm % EXPERT_ROW_GROUP == 0
    grid_spec = pltpu.PrefetchScalarGridSpec(
        num_scalar_prefetch=3,
        grid=(n_blocks + 1,),
        in_specs=[
            pl.BlockSpec((1, 1, bm), lambda j, be, nv, nu: (jnp.minimum(j, last), 0, 0), memory_space=pltpu.SMEM),
            pl.BlockSpec((1, 1, bm), lambda j, be, nv, nu: (jnp.where(j >= 2, jnp.minimum(j - 2, last), last), 0, 0),
                         memory_space=pltpu.SMEM),
            pl.BlockSpec(memory_space=pl.ANY),
            pl.BlockSpec((1, d, de), weights_of),
            pl.BlockSpec((1, d, de), weights_of),
            pl.BlockSpec((1, de, d), weights_of),
        ],
        out_specs=pl.BlockSpec(memory_space=pl.ANY),
        scratch_shapes=[pltpu.VMEM((2, bm * EXPERT_SLAB_ROWS, V7X_LANES), jnp.uint32),
                        pltpu.VMEM((2, bm * EXPERT_SLAB_ROWS, V7X_LANES), jnp.uint32),
                        pltpu.SemaphoreType.DMA,
                        pltpu.SemaphoreType.DMA],
    )
    return pl.pallas_call(
        functools.partial(_experts_kernel, bm=bm),
        out_shape=jax.ShapeDtypeStruct((TOP_K * t + bm, nl, V7X_LANES), jnp.uint32),
        grid_spec=grid_spec,
        compiler_params=_params(("arbitrary",)),
        name="routed_experts",
    )(block_expert, n_valid, n_used, tok_buf, slot_buf, h2, w_gate, w_up, w_down)


def _shared_kernel(h_ref, wg_ref, wu_ref, wd_ref, o_ref):
    x = h_ref[...]
    hid = _silu(jnp.dot(x, wg_ref[...], preferred_element_type=F32)) * jnp.dot(
        x, wu_ref[...], preferred_element_type=F32)
    o_ref[...] = jnp.dot(hid.astype(BF16), wd_ref[...], preferred_element_type=F32).astype(o_ref.dtype)


def _shared(h2, wg, wu, wd):
    t, d = h2.shape
    ds = wg.shape[1]
    tm = _pick(t, 512, 16)
    return pl.pallas_call(
        _shared_kernel,
        out_shape=jax.ShapeDtypeStruct((t, d), F32),
        grid=(t // tm,),
        in_specs=[pl.BlockSpec((tm, d), lambda i: (i, 0)),
                  pl.BlockSpec((d, ds), lambda i: (0, 0)),
                  pl.BlockSpec((d, ds), lambda i: (0, 0)),
                  pl.BlockSpec((ds, d), lambda i: (0, 0))],
        out_specs=pl.BlockSpec((tm, d), lambda i: (i, 0)),
        compiler_params=_params(("parallel",)),
        name="shared_expert",
    )(h2, wg, wu, wd)


def _combine_kernel(x_ref, wt_ref, sh_ref, mod_ref, fw_ref, *rest, gate_idx):
    yk_refs, o_ref = rest[:TOP_K], rest[TOP_K]
    wt = wt_ref[0]
    sh = sh_ref[0]
    tm, d = sh.shape
    nl = d // 2 // V7X_LANES
    lows, highs = [], []
    for c in range(nl):
        lo = sh[:, c * V7X_LANES:(c + 1) * V7X_LANES]
        hi = sh[:, d // 2 + c * V7X_LANES:d // 2 + (c + 1) * V7X_LANES]
        for kk in range(TOP_K):
            y_lo, y_hi = _unpack_halves(yk_refs[kk][pl.ds(c, tm, stride=nl), :])
            lo = lo + wt[:, kk:kk + 1] * y_lo
            hi = hi + wt[:, kk:kk + 1] * y_hi
        lows.append(lo)
        highs.append(hi)
    acc = jnp.concatenate(lows + highs, axis=1)
    m = mod_ref[0]
    x = x_ref[0] + m[gate_idx:gate_idx + 1] * acc
    o_ref[0] = x * lax.rsqrt(jnp.mean(x * x, axis=-1, keepdims=True) + EPS) * fw_ref[...]


def _combine(x1, yk, wtok, shared, mod3, final_w):
    b, n, d = x1.shape
    tm = _pick(n, 128, 16)
    nt = n // tm
    nl = d // 2 // V7X_LANES

    def yk_spec(kk):
        return pl.BlockSpec((tm * nl, V7X_LANES), lambda i, j: ((kk * b + i) * nt + j, 0))

    return pl.pallas_call(
        functools.partial(_combine_kernel, gate_idx=5),
        out_shape=jax.ShapeDtypeStruct((b, n, d), F32),
        grid=(b, nt),
        in_specs=[pl.BlockSpec((1, tm, d), lambda i, j: (i, j, 0)),
                  pl.BlockSpec((1, tm, TOP_K), lambda i, j: (i, j, 0)),
                  pl.BlockSpec((1, tm, d), lambda i, j: (i, j, 0)),
                  pl.BlockSpec((1, 6, d), lambda i, j: (i, 0, 0)),
                  pl.BlockSpec((1, d), lambda i, j: (0, 0))] + [yk_spec(kk) for kk in range(TOP_K)],
        out_specs=pl.BlockSpec((1, tm, d), lambda i, j: (i, j, 0)),
        compiler_params=_params(("parallel", "parallel")),
        name="combine_final_norm",
    )(x1, wtok, shared, mod3, final_w, *([yk] * TOP_K))


def _routing_tables(idx_kt, n_tokens):
    bm = EXPERT_BLOCK_ROWS
    n_assign = TOP_K * n_tokens
    assert n_assign % bm == 0
    n_blocks = n_assign // bm + N_EXPERTS
    e_flat = idx_kt.reshape(n_assign)
    experts = jnp.arange(N_EXPERTS, dtype=I32)
    counts = jnp.sum((e_flat[None, :] == experts[:, None]).astype(I32), axis=1)
    n_pad = (-counts) % bm
    big = 2 * N_EXPERTS
    fill = jnp.arange(bm, dtype=I32)[None, :] < n_pad[:, None]
    fill_keys = jnp.where(fill, 2 * experts[:, None] + 1, big).reshape(N_EXPERTS * bm)
    keys = jnp.concatenate([2 * e_flat, fill_keys])
    vals = jnp.concatenate([jnp.arange(n_assign, dtype=I32), jnp.full((N_EXPERTS * bm,), -1, I32)])
    keys, vals = lax.sort((keys, vals), num_keys=1)
    valid = vals >= 0
    tok = jnp.where(valid, vals % n_tokens, 0)
    slot = jnp.where(valid, vals, n_assign + jnp.arange(n_blocks * bm, dtype=I32) % bm)
    n_valid = jnp.sum(valid.reshape(n_blocks, bm).astype(I32), axis=1)
    n_used = (jnp.sum(counts + n_pad) // bm).astype(I32)
    blk = jnp.minimum(jnp.arange(n_blocks, dtype=I32), n_used - 1)
    block_expert = keys.reshape(n_blocks, bm)[:, 0][blk] // 2
    return (block_expert, n_valid, n_used.reshape(1), tok.reshape(n_blocks, 1, bm),
            slot.reshape(n_blocks, 1, bm))


def kernel(x, c, ctx, c_ctx, w_ada, b_ada, norm_mix, norm_ffn, w_in, conv_a, conv_qkv, a_log, dt_bias,
           gdn_norm, w_proj_a, w_proj_b, w_out, w_router, router_bias, w_gate, w_up, w_down,
           ws_gate, ws_up, ws_down, final_norm):
    assert w_ada.shape[0] == 1, "single (final) layer only"
    b, seq, d = x.shape
    n_ctx = ctx.shape[1]
    t = b * seq
    dc = conv_a.shape[2]
    assert seq % GRID_W == 0 and seq % CHUNK == 0 and n_ctx % CHUNK == 0
    z_off = QKV_DIM + N_GATE_COLS
    conv_off = z_off + V_DIM
    gate_off = conv_off + 3 * dc
    assert w_in.shape[2] == gate_off + 2 * d

    rows = -(-(b + 1) // 8) * 8
    cpad = jnp.zeros((rows, d), F32).at[:b].set(c).at[b].set(c_ctx)
    mod3 = _ada(cpad, w_ada[0], b_ada).reshape(rows, 6, d)

    h = _norm_mod(x, norm_mix, mod3, lambda i: i, 0, 1, BF16).reshape(t, d)
    hc = _norm_mod(ctx, norm_mix, mod3, lambda i: b, 0, 1, BF16).reshape(b * n_ctx, d)
    w_in0 = w_in[0]
    w_qkv = w_in0[:, :QKV_DIM].astype(BF16)
    w_ab = w_in0[:, QKV_DIM:z_off].astype(BF16)
    p_qkv = _mm(h, w_qkv, BF16, "inproj_qkv").reshape(b, seq, QKV_DIM)
    p_ab = _mm(h, w_ab, F32, "inproj_ab").reshape(b, seq, N_GATE_COLS)
    p_z = _mm(h, w_in0[:, z_off:conv_off].astype(BF16), BF16, "inproj_z").reshape(b, seq, V_DIM)
    p_conv = _mm(h, w_in0[:, conv_off:gate_off].astype(BF16), BF16, "inproj_conv")
    p_gate = _mm(h, w_in0[:, gate_off:].astype(BF16), BF16, "inproj_gate").reshape(b, seq, 2 * d)
    pc_qkv = _mm(hc, w_qkv, BF16, "inproj_qkv_ctx").reshape(b, n_ctx, QKV_DIM)
    pc_ab = _mm(hc, w_ab, F32, "inproj_ab_ctx").reshape(b, n_ctx, N_GATE_COLS)

    zero = jnp.zeros((N_V_HEADS,), F32)
    prm = jnp.stack([jnp.concatenate([a_log[0, 0], zero, a_log[0, 1], zero]),
                     jnp.concatenate([dt_bias[0, 0], zero, dt_bias[0, 1], zero])])
    gcol, grow = _gates(p_ab, prm)
    gcol_c, grow_c = _gates(pc_ab, prm)
    yb_in = _gdn(p_qkv, p_z, gcol, grow, pc_qkv, gcol_c, grow_c, conv_qkv[0], gdn_norm)

    ya_in = _conva(p_conv, conv_a[0])
    ya = _mm(ya_in, w_proj_a[0].astype(BF16), BF16, "proj_a").reshape(b, seq, d)
    yb = _mm(yb_in.reshape(t, V_DIM), w_proj_b[0].astype(BF16), BF16, "proj_b").reshape(b, seq, d)
    x1 = _mixout(ya, yb, p_gate, w_out[0].astype(BF16), x, mod3)

    nl = d // 2 // V7X_LANES
    h2_slabs, h2_16, idx_bkt, wgt_bkt = _router(x1, norm_ffn, mod3, w_router[0].T,
                                                router_bias[0].reshape(N_EXPERTS, 1))
    idx_kt = jnp.transpose(idx_bkt, (1, 0, 2)).reshape(TOP_K, t)
    wtok = jnp.transpose(wgt_bkt, (0, 2, 1))
    block_expert, n_valid, n_used, tok_buf, slot_buf = _routing_tables(idx_kt, t)
    yk = _experts(h2_slabs.reshape(t, nl, V7X_LANES), block_expert, n_valid, n_used, tok_buf, slot_buf,
                  w_gate[0], w_up[0], w_down[0])
    shared = _shared(h2_16.reshape(t, d), ws_gate[0].astype(BF16), ws_up[0].astype(BF16),
                     ws_down[0].astype(BF16)).reshape(b, seq, d)
    return _combine(x1, yk.reshape(-1, V7X_LANES), wtok, shared, mod3, final_norm.reshape(1, d))
```

```python
import functools

import jax
import jax.numpy as jnp
from jax import lax
from jax.experimental import pallas as pl
from jax.experimental.pallas import tpu as pltpu

F32 = jnp.float32
BF16 = jnp.bfloat16
I32 = jnp.int32

EPS = 1e-6
GRID_W = 64
N_QK_HEADS = 16
N_V_HEADS = 32
HEAD_K = 128
HEAD_V = 128
CHUNK = 64
N_EXPERTS = 256
N_GROUPS = 8
TOPK_GROUPS = 4
TOP_K = 8
ROUTED_SCALE = 2.5

QK_DIM = N_QK_HEADS * HEAD_K
V_DIM = N_V_HEADS * HEAD_V
QKV_DIM = 2 * QK_DIM + V_DIM
N_GATE_COLS = 4 * N_V_HEADS

V7X_LANES = 128
V7X_VMEM_LIMIT_BYTES = 56 * 1024 * 1024
V7X_MXU_COLS = 256
EXPERT_BLOCK_ROWS = 256
EXPERT_ROW_GROUP = 8
EXPERT_SLAB_ROWS = 9
GDN_LOCAL_GROUP = 4


def _pick(n, target, mult):
    if n <= target:
        return n
    t = (target // mult) * mult
    while t >= mult:
        if n % t == 0:
            return t
        t -= mult
    raise ValueError(f"no tile for {n} (target {target}, multiple {mult})")


def _params(sem, vmem=V7X_VMEM_LIMIT_BYTES):
    return pltpu.CompilerParams(dimension_semantics=sem, vmem_limit_bytes=vmem)


def _sigmoid(x):
    return 1.0 / (1.0 + jnp.exp(-x))


def _silu(x):
    return x * _sigmoid(x)


def _dot(a, b):
    return jnp.dot(a.astype(BF16), b.astype(BF16), preferred_element_type=F32)


def _dot_nt(a, b):
    return lax.dot_general(a.astype(BF16), b.astype(BF16), (((1,), (1,)), ((), ())),
                           preferred_element_type=F32)


def _dot_tn(a, b):
    return lax.dot_general(a.astype(BF16), b.astype(BF16), (((0,), (0,)), ((), ())),
                           preferred_element_type=F32)


def _pack_halves(x):
    n = x.shape[1] // 2
    bits = lax.bitcast_convert_type(x.astype(BF16).astype(F32), jnp.uint32)
    return bits[:, n:] | lax.shift_right_logical(bits[:, :n], jnp.uint32(16))


def _unpack_halves(w):
    lo = lax.bitcast_convert_type(lax.shift_left(w, jnp.uint32(16)), F32)
    hi = lax.bitcast_convert_type(w & jnp.uint32(0xFFFF0000), F32)
    return lo, hi


def _split3(x):
    hi = x.astype(BF16)
    r = x - hi.astype(F32)
    mid = r.astype(BF16)
    lo = (r - mid.astype(F32)).astype(BF16)
    return hi, mid, lo


def _ada_kernel(c_ref, w_ref, b_ref, o_ref):
    a = _silu(c_ref[...])
    o_ref[...] = _dot(a, w_ref[...]) + b_ref[...]


def _ada(cpad, w, b):
    rows, d = cpad.shape
    n = w.shape[1]
    tn = _pick(n, 1024, V7X_LANES)
    return pl.pallas_call(
        _ada_kernel,
        out_shape=jax.ShapeDtypeStruct((rows, n), F32),
        grid=(n // tn,),
        in_specs=[pl.BlockSpec((rows, d), lambda j: (0, 0)),
                  pl.BlockSpec((d, tn), lambda j: (0, j)),
                  pl.BlockSpec((1, tn), lambda j: (0, j))],
        out_specs=pl.BlockSpec((rows, tn), lambda j: (0, j)),
        compiler_params=_params(("parallel",)),
        name="ada_mod",
    )(cpad, w, b)


def _norm_mod_kernel(x_ref, w_ref, mod_ref, o_ref, *, shift_idx, scale_idx):
    x = x_ref[0]
    y = x * lax.rsqrt(jnp.mean(x * x, axis=-1, keepdims=True) + EPS) * w_ref[...]
    m = mod_ref[0]
    o_ref[0] = (y * (1.0 + m[scale_idx:scale_idx + 1]) + m[shift_idx:shift_idx + 1]).astype(o_ref.dtype)


def _norm_mod(x, w, mod3, row_of_batch, shift_idx, scale_idx, out_dtype):
    b, n, d = x.shape
    ts = _pick(n, 512, 16)
    return pl.pallas_call(
        functools.partial(_norm_mod_kernel, shift_idx=shift_idx, scale_idx=scale_idx),
        out_shape=jax.ShapeDtypeStruct((b, n, d), out_dtype),
        grid=(b, n // ts),
        in_specs=[pl.BlockSpec((1, ts, d), lambda i, j: (i, j, 0)),
                  pl.BlockSpec((1, d), lambda i, j: (0, 0)),
                  pl.BlockSpec((1, 6, d), lambda i, j: (row_of_batch(i), 0, 0))],
        out_specs=pl.BlockSpec((1, ts, d), lambda i, j: (i, j, 0)),
        compiler_params=_params(("parallel", "parallel")),
        name="norm_mod",
    )(x, w, mod3)


def _mm_kernel(a_ref, w_ref, o_ref):
    o_ref[...] = jnp.dot(a_ref[...], w_ref[...], preferred_element_type=F32).astype(o_ref.dtype)


def _mm(a, w, out_dtype, name):
    m, k = a.shape
    n = w.shape[1]
    tm = _pick(m, 1024, 16)
    tn = _pick(n, 1024, V7X_LANES)
    return pl.pallas_call(
        _mm_kernel,
        out_shape=jax.ShapeDtypeStruct((m, n), out_dtype),
        grid=(n // tn, m // tm),
        in_specs=[pl.BlockSpec((tm, k), lambda j, i: (i, 0)),
                  pl.BlockSpec((k, tn), lambda j, i: (0, j))],
        out_specs=pl.BlockSpec((tm, tn), lambda j, i: (i, j)),
        compiler_params=_params(("parallel", "parallel")),
        name=name,
    )(a, w)


def _gates_kernel(ab_ref, prm_ref, col_ref, row_ref, *, rows):
    sub = 2 * CHUNK
    n_sub = rows // sub
    prm = prm_ref[...]
    alog = prm[0:1]
    dtb = prm[1:2]
    lane = lax.broadcasted_iota(I32, (sub, N_GATE_COLS), 1)
    is_g = (lane // N_V_HEADS) % 2 == 0
    is_bwd = lane >= 2 * N_V_HEADS
    r = lax.broadcasted_iota(I32, (sub, sub), 0)
    c = lax.broadcasted_iota(I32, (sub, sub), 1)
    same = (r // CHUNK) == (c // CHUNK)
    cum_f = jnp.where(same & (c <= r), 1.0, 0.0).astype(BF16)
    cum_b = jnp.where(same & (c >= r), 1.0, 0.0).astype(BF16)
    for s in range(n_sub):
        x = ab_ref[0, s * sub:(s + 1) * sub, :]
        xs = x + dtb
        softplus = jnp.maximum(xs, 0.0) + jnp.log(1.0 + jnp.exp(-jnp.abs(xs)))
        g = -jnp.exp(alog) * softplus
        beta = _sigmoid(x)
        hi, mid, lo = _split3(g)
        pf = (jnp.dot(cum_f, hi, preferred_element_type=F32) + jnp.dot(cum_f, mid, preferred_element_type=F32)
              + jnp.dot(cum_f, lo, preferred_element_type=F32))
        pb = (jnp.dot(cum_b, hi, preferred_element_type=F32) + jnp.dot(cum_b, mid, preferred_element_type=F32)
              + jnp.dot(cum_b, lo, preferred_element_type=F32))
        out = jnp.where(is_g, jnp.where(is_bwd, pb, pf), beta)
        col_ref[0, s * sub:(s + 1) * sub, :] = out
        for h in range(2):
            oc = out[h * CHUNK:(h + 1) * CHUNK]
            row_ref[0, 2 * s + h] = jnp.concatenate([oc, oc], axis=0).T


def _gates(ab, prm):
    b, n, g = ab.shape
    rows = _pick(n, 512, 2 * CHUNK)
    return pl.pallas_call(
        functools.partial(_gates_kernel, rows=rows),
        out_shape=(jax.ShapeDtypeStruct((b, n, g), F32),
                   jax.ShapeDtypeStruct((b, n // CHUNK, g, 2 * CHUNK), F32)),
        grid=(b, n // rows),
        in_specs=[pl.BlockSpec((1, rows, g), lambda i, j: (i, j, 0)),
                  pl.BlockSpec((2, g), lambda i, j: (0, 0))],
        out_specs=(pl.BlockSpec((1, rows, g), lambda i, j: (i, j, 0)),
                   pl.BlockSpec((1, rows // CHUNK, g, 2 * CHUNK), lambda i, j: (i, j, 0, 0))),
        compiler_params=_params(("parallel", "parallel")),
        name="gate_prep",
    )(ab, prm)


def _conv_silu_tile(src_ref, w, base, rows, n_total):
    x = src_ref[0, pl.ds(base, rows), :].astype(F32)
    pb = pl.multiple_of(jnp.maximum(base - 16, 0), 16)
    nb = pl.multiple_of(jnp.minimum(base + rows, n_total - 16), 16)
    prev = src_ref[0, pl.ds(pb, 16), :].astype(F32)[15:16]
    nxt = src_ref[0, pl.ds(nb, 16), :].astype(F32)[0:1]
    prev = jnp.where(base > 0, prev, 0.0)
    nxt = jnp.where(base + rows < n_total, nxt, 0.0)
    ridx = lax.broadcasted_iota(I32, x.shape, 0)
    xp = jnp.where(ridx == 0, prev, pltpu.roll(x, 1, 0))
    xn = jnp.where(ridx == rows - 1, nxt, pltpu.roll(x, rows - 1, 0))
    y = xp * w[0:1] + x * w[1:2] + xn * w[2:3]
    return _silu(y)


def _l2norm_rows(t):
    return t * lax.rsqrt(jnp.sum(t * t, axis=-1, keepdims=True) + EPS)


def _gdn_kernel(q_ref, k_ref, v_ref, z_ref, gc_ref, gt_ref,
                qc_ref, kc_ref, vc_ref, gcc_ref, gtc_ref,
                cq_ref, ck_ref, cv_ref, gn_ref,
                y_ref,
                qs, ks, vs, osum, st, u_s, w_s, *, n_lat, n_ctx):
    hq = pl.program_id(1)
    rep = N_V_HEADS // N_QK_HEADS
    wq = cq_ref[...]
    wk = ck_ref[...]
    wv = cv_ref[...]

    def prep(src_q, src_k, src_v, n_total, row0):
        tile = _pick(n_total, 256, 16)

        def body(t, carry):
            base = pl.multiple_of(t * tile, tile)
            dst = pl.multiple_of(row0 + base, 16)
            qs[pl.ds(dst, tile), :] = _l2norm_rows(_conv_silu_tile(src_q, wq, base, tile, n_total)) * (HEAD_K ** -0.5)
            ks[pl.ds(dst, tile), :] = _l2norm_rows(_conv_silu_tile(src_k, wk, base, tile, n_total))
            vs[pl.ds(dst, tile), :] = _conv_silu_tile(src_v, wv, base, tile, n_total)
            return carry

        lax.fori_loop(0, n_total // tile, body, 0)

    prep(qc_ref, kc_ref, vc_ref, n_ctx, 0)
    prep(q_ref, k_ref, v_ref, n_lat, n_ctx)

    st[...] = jnp.zeros(st.shape, F32)
    osum[...] = jnp.zeros(osum.shape, F32)

    wide = (CHUNK, 2 * CHUNK)
    ri = lax.broadcasted_iota(I32, wide, 0)
    li = lax.broadcasted_iota(I32, wide, 1)
    ci = li % CHUNK
    left = li < CHUNK
    eye_left = jnp.where(left & (ri == ci), 1.0, 0.0)
    incl = (ri >= ci, ri <= ci)
    strict = (ri > ci, ri < ci)
    last = (CHUNK - 1, 0)
    lane = lax.broadcasted_iota(I32, (CHUNK, N_GATE_COLS), 1)
    zeros_sq = jnp.zeros(wide, BF16)
    zeros_uw = jnp.zeros((CHUNK, 2 * HEAD_V), BF16)

    def gate_col(gtile, col):
        return jnp.sum(jnp.where(lane == col, gtile, 0.0), axis=1, keepdims=True)

    def decay_of(gcol, grow2, dirn):
        return jnp.where(incl[dirn], jnp.exp(jnp.where(incl[dirn], gcol - grow2, 0.0)), 0.0)

    def mm(a, b):
        return jnp.dot(a, b, preferred_element_type=F32)

    grp = GDN_LOCAL_GROUP

    def chunk_of(seq, g, j, dirn):
        n_chunks = seq[3]
        return grp * g + j if dirn == 0 else n_chunks - 1 - grp * g - j

    def local_stages(seq, g, slot):
        gcol_ref, grow_ref, row0, _ = seq
        box = {}

        def prep():
            sbs, rhss, dests = [], [], []
            for j in range(grp):
                for dirn in range(2):
                    c = chunk_of(seq, g, j, dirn)
                    r = pl.multiple_of(row0 + c * CHUNK, CHUNK)
                    k = ks[pl.ds(r, CHUNK), :]
                    vv = vs[pl.ds(r, CHUNK), :]
                    k16 = k.astype(BF16)
                    kk2 = _dot_nt(k16, jnp.concatenate([k16, k16], axis=0))
                    gtile = gcol_ref[0, pl.ds(pl.multiple_of(c * CHUNK, CHUNK), CHUNK), :]
                    for head in range(rep):
                        col_g = (2 * dirn) * N_V_HEADS + hq * rep + head
                        gcol = gate_col(gtile, col_g)
                        bcol = gate_col(gtile, col_g + N_V_HEADS)
                        grow2 = grow_ref[0, c, 2 * dirn, pl.ds(head, 1), :]
                        a2 = bcol * kk2 * decay_of(gcol, grow2, dirn)
                        sbs.append(jnp.where(left, eye_left, jnp.where(strict[dirn], -a2, 0.0)))
                        v = vv[:, head * HEAD_V:(head + 1) * HEAD_V]
                        rhss.append(jnp.concatenate([v * bcol, k * (bcol * jnp.exp(gcol))], axis=1).astype(BF16))
                        dests.append((dirn * rep + head, j))
            box.update(sbs=sbs, rhss=rhss, dests=dests)

        def double():
            nxt = []
            for sb in box["sbs"]:
                sb16 = sb.astype(BF16)
                nxt.append(mm(sb16, jnp.concatenate([zeros_sq, sb16], axis=0)) + jnp.where(left, sb, 0.0))
            box["sbs"] = nxt

        def apply_s32():
            box["sb16s"] = [sb.astype(BF16) for sb in box["sbs"]]
            box["xs"] = [mm(sb16, jnp.concatenate([rhs, zeros_uw], axis=0))
                         for sb16, rhs in zip(box["sb16s"], box["rhss"])]

        def finish():
            for sb16, x, (chain, j) in zip(box["sb16s"], box["xs"], box["dests"]):
                uw = x + mm(sb16, jnp.concatenate([zeros_uw, x.astype(BF16)], axis=0))
                u_s[slot, chain, j * CHUNK:(j + 1) * CHUNK, :] = uw[:, :HEAD_V]
                w_s[slot, chain, j * CHUNK:(j + 1) * CHUNK, :] = uw[:, HEAD_V:].astype(BF16)

        return [prep] + [double] * 5 + [apply_s32, finish]

    def scan_stages(seq, g, j, slot, write_out):
        gcol_ref, grow_ref, row0, _ = seq
        box = {}

        def inputs():
            work = []
            for dirn in range(2):
                c = chunk_of(seq, g, j, dirn)
                r = pl.multiple_of(row0 + c * CHUNK, CHUNK)
                lr = pl.multiple_of(c * CHUNK, CHUNK)
                k = ks[pl.ds(r, CHUNK), :]
                gtile = gcol_ref[0, pl.ds(lr, CHUNK), :]
                if write_out:
                    q = qs[pl.ds(r, CHUNK), :]
                    k16 = k.astype(BF16)
                    qk2 = _dot_nt(q, jnp.concatenate([k16, k16], axis=0))
                for head in range(rep):
                    chain = dirn * rep + head
                    gcol = gate_col(gtile, (2 * dirn) * N_V_HEADS + hq * rep + head)
                    glast = gcol[last[dirn]:last[dirn] + 1, :]
                    kdec16 = (k * jnp.exp(glast - gcol)).astype(BF16)
                    lhs = w_s[slot, chain, j * CHUNK:(j + 1) * CHUNK, :]
                    p16 = None
                    if write_out:
                        grow2 = grow_ref[0, c, 2 * dirn, pl.ds(head, 1), :]
                        p16 = (qk2 * decay_of(gcol, grow2, dirn)).astype(BF16)
                        lhs = jnp.concatenate([lhs, (q * jnp.exp(gcol)).astype(BF16)], axis=0)
                    work.append((chain, lr, head, glast, kdec16, lhs, p16))
            box["work"] = work

        def state_matmul():
            box["s_olds"] = [st[wk[0]] for wk in box["work"]]
            box["tbs"] = [mm(wk[5], s_old.astype(BF16))
                          for wk, s_old in zip(box["work"], box["s_olds"])]

        def update():
            for (chain, lr, head, glast, kdec16, _, p16), s_old, tb in zip(box["work"], box["s_olds"], box["tbs"]):
                vn16 = (u_s[slot, chain, j * CHUNK:(j + 1) * CHUNK, :] - tb[:CHUNK]).astype(BF16)
                if write_out:
                    o = tb[CHUNK:] + mm(p16, jnp.concatenate([vn16, zeros_sq], axis=0))
                    osum[pl.ds(lr, CHUNK), head * HEAD_V:(head + 1) * HEAD_V] += o
                st[chain] = s_old * jnp.exp(glast) + _dot_tn(kdec16, vn16)

        return [inputs, state_matmul, update]

    def run_group(scan_of, local_of):
        local = local_stages(*local_of) if local_of else []
        steps = [scan_stages(scan_of[0], scan_of[1], j, scan_of[2], scan_of[3]) for j in range(grp)] if scan_of else []
        pending = list(local)
        for step in steps:
            step[0]()
            for stage in step[1:]:
                if pending:
                    pending.pop(0)()
                stage()
        for stage in pending:
            stage()

    ctx_seq = (gcc_ref, gtc_ref, 0, n_ctx // CHUNK)
    lat_seq = (gc_ref, gt_ref, n_ctx, n_lat // CHUNK)
    ng_ctx, ng_lat = n_ctx // CHUNK // grp, n_lat // CHUNK // grp
    run_group(None, (ctx_seq, 0, 0))
    for g in range(ng_ctx - 1):
        run_group((ctx_seq, g, g % 2, False), (ctx_seq, g + 1, (g + 1) % 2))
    run_group((ctx_seq, ng_ctx - 1, (ng_ctx - 1) % 2, False), (lat_seq, 0, ng_ctx % 2))

    def lat_body(g, carry):
        run_group((lat_seq, g, (ng_ctx + g) % 2, True), (lat_seq, g + 1, (ng_ctx + g + 1) % 2))
        return carry

    lax.fori_loop(0, ng_lat - 1, lat_body, 0)
    run_group((lat_seq, ng_lat - 1, (ng_ctx + ng_lat - 1) % 2, True), None)

    gn = gn_ref[...]
    tile = _pick(n_lat, 256, 16)

    def epi(t, carry):
        base = pl.multiple_of(t * tile, tile)
        o = osum[pl.ds(base, tile), :]
        z = z_ref[0, pl.ds(base, tile), :].astype(F32)
        for head in range(rep):
            sl = slice(head * HEAD_V, (head + 1) * HEAD_V)
            oh = o[:, sl]
            y = oh * lax.rsqrt(jnp.mean(oh * oh, axis=-1, keepdims=True) + EPS) * gn * _silu(z[:, sl])
            y_ref[0, pl.ds(base, tile), sl] = y.astype(y_ref.dtype)
        return carry

    lax.fori_loop(0, n_lat // tile, epi, 0)


def _gdn(p_qkv, p_z, gcol, grow, pc_qkv, gcol_c, grow_c, conv_qkv, gdn_norm):
    b, n_lat, _ = p_qkv.shape
    n_ctx = pc_qkv.shape[1]
    rep = N_V_HEADS // N_QK_HEADS
    vw = rep * HEAD_V
    kq0 = QK_DIM // HEAD_K
    v0 = 2 * QK_DIM // vw
    nc_lat, nc_ctx = n_lat // CHUNK, n_ctx // CHUNK
    assert nc_lat % GDN_LOCAL_GROUP == 0 and nc_ctx % GDN_LOCAL_GROUP == 0
    assert HEAD_K == HEAD_V == N_GATE_COLS == 2 * CHUNK
    grow5 = grow.reshape(b, nc_lat, 4, N_QK_HEADS, rep, 2 * CHUNK)
    grow5_c = grow_c.reshape(b, nc_ctx, 4, N_QK_HEADS, rep, 2 * CHUNK)

    def seq_specs(n):
        return [pl.BlockSpec((1, n, HEAD_K), lambda i, h: (i, 0, h)),
                pl.BlockSpec((1, n, HEAD_K), lambda i, h: (i, 0, kq0 + h)),
                pl.BlockSpec((1, n, vw), lambda i, h: (i, 0, v0 + h))]

    def gate_specs(n, nc):
        return [pl.BlockSpec((1, n, N_GATE_COLS), lambda i, h: (i, 0, 0)),
                pl.BlockSpec((1, nc, 4, None, rep, 2 * CHUNK), lambda i, h: (i, 0, 0, h, 0, 0))]

    in_specs = (seq_specs(n_lat)
                + [pl.BlockSpec((1, n_lat, vw), lambda i, h: (i, 0, h))]
                + gate_specs(n_lat, nc_lat)
                + seq_specs(n_ctx)
                + gate_specs(n_ctx, nc_ctx)
                + [pl.BlockSpec((3, HEAD_K), lambda i, h: (0, h)),
                   pl.BlockSpec((3, HEAD_K), lambda i, h: (0, kq0 + h)),
                   pl.BlockSpec((3, vw), lambda i, h: (0, v0 + h)),
                   pl.BlockSpec((1, HEAD_V), lambda i, h: (0, 0))])
    n_all = n_ctx + n_lat
    return pl.pallas_call(
        functools.partial(_gdn_kernel, n_lat=n_lat, n_ctx=n_ctx),
        out_shape=jax.ShapeDtypeStruct((b, n_lat, V_DIM), BF16),
        grid=(b, N_QK_HEADS),
        in_specs=in_specs,
        out_specs=pl.BlockSpec((1, n_lat, vw), lambda i, h: (i, 0, h)),
        scratch_shapes=[pltpu.VMEM((n_all, HEAD_K), F32),
                        pltpu.VMEM((n_all, HEAD_K), F32),
                        pltpu.VMEM((n_all, vw), F32),
                        pltpu.VMEM((n_lat, vw), F32),
                        pltpu.VMEM((2 * rep, HEAD_K, HEAD_V), F32),
                        pltpu.VMEM((2, 2 * rep, GDN_LOCAL_GROUP * CHUNK, HEAD_V), F32),
                        pltpu.VMEM((2, 2 * rep, GDN_LOCAL_GROUP * CHUNK, HEAD_K), BF16)],
        compiler_params=_params(("parallel", "parallel")),
        name="gdn_bidir",
    )(p_qkv, p_qkv, p_qkv, p_z, gcol, grow5,
      pc_qkv, pc_qkv, pc_qkv, gcol_c, grow5_c,
      conv_qkv, conv_qkv, conv_qkv, gdn_norm)


def _conva_kernel(gb_ref, gcx_ref, xin_ref, w_ref, o_ref):
    u = gcx_ref[...].astype(F32) * xin_ref[...].astype(F32)
    rows = u.shape[0]
    pos = lax.broadcasted_iota(I32, u.shape, 0) % GRID_W
    up = jnp.where(pos == 0, 0.0, pltpu.roll(u, 1, 0))
    un = jnp.where(pos == GRID_W - 1, 0.0, pltpu.roll(u, rows - 1, 0))
    w = w_ref[...]
    conv = up * w[0:1] + u * w[1:2] + un * w[2:3]
    o_ref[...] = (gb_ref[...].astype(F32) * conv).astype(o_ref.dtype)


def _conva(p_conv, conv_w):
    t, c3 = p_conv.shape
    dc = c3 // 3
    tr = _pick(t, 512, GRID_W)
    tc = _pick(dc, 512, V7X_LANES)
    nj = dc // tc
    return pl.pallas_call(
        _conva_kernel,
        out_shape=jax.ShapeDtypeStruct((t, dc), BF16),
        grid=(t // tr, nj),
        in_specs=[pl.BlockSpec((tr, tc), lambda i, j: (i, j)),
                  pl.BlockSpec((tr, tc), lambda i, j: (i, nj + j)),
                  pl.BlockSpec((tr, tc), lambda i, j: (i, 2 * nj + j)),
                  pl.BlockSpec((3, tc), lambda i, j: (0, j))],
        out_specs=pl.BlockSpec((tr, tc), lambda i, j: (i, j)),
        compiler_params=_params(("parallel", "parallel")),
        name="mixer_a_conv",
    )(p_conv, p_conv, p_conv, conv_w)


def _mixout_kernel(ya_ref, yb_ref, ga_ref, gb_ref, wo_ref, x_ref, mod_ref, o_ref, *, gate_idx):
    mix = (_sigmoid(ga_ref[0].astype(F32)) * ya_ref[0].astype(F32)
           + _sigmoid(gb_ref[0].astype(F32)) * yb_ref[0].astype(F32))
    out = _dot(mix, wo_ref[...])
    m = mod_ref[0]
    o_ref[0] = x_ref[0] + m[gate_idx:gate_idx + 1] * out


def _mixout(ya, yb, p_gate, w_o, x, mod3):
    b, n, d = x.shape
    tm = _pick(n, 512, 16)
    return pl.pallas_call(
        functools.partial(_mixout_kernel, gate_idx=2),
        out_shape=jax.ShapeDtypeStruct((b, n, d), F32),
        grid=(b, n // tm),
        in_specs=[pl.BlockSpec((1, tm, d), lambda i, j: (i, j, 0)),
                  pl.BlockSpec((1, tm, d), lambda i, j: (i, j, 0)),
                  pl.BlockSpec((1, tm, d), lambda i, j: (i, j, 0)),
                  pl.BlockSpec((1, tm, d), lambda i, j: (i, j, 1)),
                  pl.BlockSpec((d, d), lambda i, j: (0, 0)),
                  pl.BlockSpec((1, tm, d), lambda i, j: (i, j, 0)),
                  pl.BlockSpec((1, 6, d), lambda i, j: (i, 0, 0))],
        out_specs=pl.BlockSpec((1, tm, d), lambda i, j: (i, j, 0)),
        compiler_params=_params(("parallel", "parallel")),
        name="merge_outproj",
    )(ya, yb, p_gate, p_gate, w_o, x, mod3)


def _router_kernel(x_ref, w_ref, mod_ref, wr_ref, bias_ref, hs_ref, h16_ref, idx_ref, wgt_ref, *,
                   shift_idx, scale_idx):
    x = x_ref[0]
    y = x * lax.rsqrt(jnp.mean(x * x, axis=-1, keepdims=True) + EPS) * w_ref[...]
    m = mod_ref[0]
    h = y * (1.0 + m[scale_idx:scale_idx + 1]) + m[shift_idx:shift_idx + 1]
    tm, d = h.shape
    h_hi = h.astype(BF16)
    h16_ref[0] = h_hi
    packed = _pack_halves(h_hi)
    npk = d // 2 // V7X_LANES
    for c in range(npk):
        hs_ref[0, pl.ds(c, tm, stride=npk), :] = packed[:, c * V7X_LANES:(c + 1) * V7X_LANES]
    h_lo = (h - h_hi.astype(F32)).astype(BF16)
    wr = wr_ref[...]
    w_hi = wr.astype(BF16)
    w_lo = (wr - w_hi.astype(F32)).astype(BF16)
    logits = _dot_nt(w_hi, h_hi) + _dot_nt(w_hi, h_lo) + _dot_nt(w_lo, h_hi)
    scores = _sigmoid(logits)
    biased = scores + bias_ref[...]
    per_group = N_EXPERTS // N_GROUPS
    neg = -jnp.inf
    i32g = lax.broadcasted_iota(I32, (per_group, tm), 0).astype(F32)
    groups, gscore = [], []
    for g in range(N_GROUPS):
        sg = biased[g * per_group:(g + 1) * per_group, :]
        m1 = jnp.max(sg, axis=0, keepdims=True)
        a1 = jnp.min(jnp.where(sg == m1, i32g, float(per_group)), axis=0, keepdims=True)
        m2 = jnp.max(jnp.where(i32g == a1, neg, sg), axis=0, keepdims=True)
        groups.append(sg)
        gscore.append(m1 + m2)
    parts = []
    for g in range(N_GROUPS):
        ahead = jnp.zeros((1, tm), F32)
        for o in range(N_GROUPS):
            if o == g:
                continue
            beats = (gscore[o] >= gscore[g]) if o < g else (gscore[o] > gscore[g])
            ahead = ahead + jnp.where(beats, 1.0, 0.0)
        parts.append(jnp.where(ahead < float(TOPK_GROUPS), groups[g], neg))
    masked = jnp.concatenate(parts, axis=0)
    ei = lax.broadcasted_iota(I32, masked.shape, 0).astype(F32)
    idxs, wgts = [], []
    for _ in range(TOP_K):
        em = jnp.max(masked, axis=0, keepdims=True)
        ea = jnp.min(jnp.where(masked == em, ei, float(N_EXPERTS)), axis=0, keepdims=True)
        hit = ei == ea
        idxs.append(ea)
        wgts.append(jnp.sum(jnp.where(hit, scores, 0.0), axis=0, keepdims=True))
        masked = jnp.where(hit, neg, masked)
    idx = jnp.concatenate(idxs, axis=0)
    wgt = jnp.concatenate(wgts, axis=0)
    wgt = wgt / jnp.sum(wgt, axis=0, keepdims=True) * ROUTED_SCALE
    idx_ref[0] = idx.astype(I32)
    wgt_ref[0] = wgt


def _router(x1, norm_w, mod3, wr_t, bias_col):
    b, n, d = x1.shape
    tm = _pick(n, 256, V7X_LANES)
    nt = n // tm
    nl = d // 2 // V7X_LANES
    return pl.pallas_call(
        functools.partial(_router_kernel, shift_idx=3, scale_idx=4),
        out_shape=(jax.ShapeDtypeStruct((b, n * nl, V7X_LANES), jnp.uint32),
                   jax.ShapeDtypeStruct((b, n, d), BF16),
                   jax.ShapeDtypeStruct((b, TOP_K, n), I32),
                   jax.ShapeDtypeStruct((b, TOP_K, n), F32)),
        grid=(b, nt),
        in_specs=[pl.BlockSpec((1, tm, d), lambda i, j: (i, j, 0)),
                  pl.BlockSpec((1, d), lambda i, j: (0, 0)),
                  pl.BlockSpec((1, 6, d), lambda i, j: (i, 0, 0)),
                  pl.BlockSpec((N_EXPERTS, d), lambda i, j: (0, 0)),
                  pl.BlockSpec((N_EXPERTS, 1), lambda i, j: (0, 0))],
        out_specs=(pl.BlockSpec((1, tm * nl, V7X_LANES), lambda i, j: (i, j, 0)),
                   pl.BlockSpec((1, tm, d), lambda i, j: (i, j, 0)),
                   pl.BlockSpec((1, TOP_K, tm), lambda i, j: (i, 0, j)),
                   pl.BlockSpec((1, TOP_K, tm), lambda i, j: (i, 0, j))),
        compiler_params=_params(("parallel", "parallel")),
        name="ffn_norm_router",
    )(x1, norm_w, mod3, wr_t, bias_col)


def _spread(total, weights):
    acc, out, lo = 0.0, [], 0
    for w in weights:
        acc += w
        hi = round(total * acc / sum(weights))
        out.append(range(lo, hi))
        lo = hi
    return out


def _experts_kernel(be_ref, nv_ref, nu_ref, tok_ref, slot_ref, h_hbm, wg_ref, wu_ref, wd_ref, y_hbm,
                    xbuf, ybuf, gsem, ssem, *, bm):
    j = pl.program_id(0)
    n_used = nu_ref[0]
    nl = h_hbm.shape[1]
    d = 2 * nl * V7X_LANES
    de = wg_ref.shape[2]
    mxu = V7X_MXU_COLS
    pitch = EXPERT_SLAB_ROWS

    rg = EXPERT_ROW_GROUP
    n_blocks = nv_ref.shape[0]

    def rows_up(n):
        return (n + (rg - 1)) & ~(rg - 1)

    def gather_rows(s):
        return rows_up(jnp.where(s < n_used, nv_ref[jnp.clip(s, 0, n_blocks - 1)], 0))

    def scatter_rows(s):
        real = jnp.where((s >= 2) & (s - 2 < n_used), nv_ref[jnp.clip(s - 2, 0, n_blocks - 1)], 0)
        return rows_up(jnp.where(s == 0, bm, real))

    def wait_step_dmas(s):
        for n, sem in ((gather_rows(s), gsem), (scatter_rows(s), ssem)):
            @pl.when(n > 0)
            def _(n=n, sem=sem):
                pltpu.make_async_copy(h_hbm.at[pl.ds(0, n)], y_hbm.at[pl.ds(0, n)], sem).wait()

    @pl.when(j == 0)
    def _():
        xbuf[...] = jnp.zeros(xbuf.shape, xbuf.dtype)
        ybuf[...] = jnp.zeros(ybuf.shape, ybuf.dtype)

    @pl.when((j >= 1) & (j <= n_used + 1))
    def _():
        wait_step_dmas(j - 1)

    def step(nxt):
        cur = 1 - nxt
        n_gather, n_scatter = gather_rows(j), scatter_rows(j)

        def move_rows(groups):
            for g in groups:
                @pl.when(g * rg < n_gather)
                def _(g=g):
                    for r in range(g * rg, (g + 1) * rg):
                        pltpu.make_async_copy(h_hbm.at[tok_ref[0, 0, r]],
                                              xbuf.at[nxt, pl.ds(r * pitch, nl)], gsem).start()

                @pl.when(g * rg < n_scatter)
                def _(g=g):
                    for r in range(g * rg, (g + 1) * rg):
                        pltpu.make_async_copy(ybuf.at[nxt, pl.ds(r * pitch, nl)],
                                              y_hbm.at[slot_ref[0, 0, r]], ssem).start(priority=1)

        n_up, n_down = de // mxu, d // mxu
        lanes_per_slice = mxu // V7X_LANES
        parts = _spread(bm // rg, [float(d)] * (2 * n_up) + [float(de)] * n_down)
        halves = [_unpack_halves(xbuf[cur, pl.ds(c, bm, stride=pitch), :]) for c in range(nl)]
        x = jnp.concatenate([lo for lo, _ in halves] + [hi for _, hi in halves], axis=1).astype(BF16)
        hid = []
        for s in range(n_up):
            cols = slice(s * mxu, (s + 1) * mxu)
            g = jnp.dot(x, wg_ref[0, :, cols].astype(BF16), preferred_element_type=F32)
            move_rows(parts[2 * s])
            u = jnp.dot(x, wu_ref[0, :, cols].astype(BF16), preferred_element_type=F32)
            move_rows(parts[2 * s + 1])
            hid.append((_silu(g) * u).astype(BF16))
        hid = jnp.concatenate(hid, axis=1)
        for s in range(n_down // 2):
            y_lo = jnp.dot(hid, wd_ref[0, :, s * mxu:(s + 1) * mxu].astype(BF16), preferred_element_type=F32)
            move_rows(parts[2 * n_up + 2 * s])
            y_hi = jnp.dot(hid, wd_ref[0, :, d // 2 + s * mxu:d // 2 + (s + 1) * mxu].astype(BF16),
                           preferred_element_type=F32)
            packed = _pack_halves(jnp.concatenate([y_lo, y_hi], axis=1))
            for c in range(lanes_per_slice):
                ybuf[cur, pl.ds(s * lanes_per_slice + c, bm, stride=pitch), :] = (
                    packed[:, c * V7X_LANES:(c + 1) * V7X_LANES])
            move_rows(parts[2 * n_up + 2 * s + 1])

    for parity in range(2):
        @pl.when((j <= n_used + 1) & (j % 2 == parity))
        def _(parity=parity):
            step(parity)

    @pl.when(j == n_used + 1)
    def _():
        wait_step_dmas(j)


def _experts(h2, block_expert, n_valid, n_used, tok_buf, slot_buf, w_gate, w_up, w_down):
    t, nl, _ = h2.shape
    d = 2 * nl * V7X_LANES
    de = w_gate.shape[2]
    n_blocks, _, bm = tok_buf.shape
    assert de % V7X_MXU_COLS == 0 and d % (2 * V7X_MXU_COLS) == 0 and EXPERT_SLAB_ROWS >= nl
    last = n_blocks - 1

    def weights_of(j, be, nv, nu):
        return (be[jnp.maximum(j - 1, 0)], 0, 0)

    assert bm % EXPERT_ROW_GROUP == 0
    grid_spec = pltpu.PrefetchScalarGridSpec(
        num_scalar_prefetch=3,
        grid=(n_blocks + 1,),
        in_specs=[
            pl.BlockSpec((1, 1, bm), lambda j, be, nv, nu: (jnp.minimum(j, last), 0, 0), memory_space=pltpu.SMEM),
            pl.BlockSpec((1, 1, bm), lambda j, be, nv, nu: (jnp.where(j >= 2, jnp.minimum(j - 2, last), last), 0, 0),
                         memory_space=pltpu.SMEM),
            pl.BlockSpec(memory_space=pl.ANY),
            pl.BlockSpec((1, d, de), weights_of),
            pl.BlockSpec((1, d, de), weights_of),
            pl.BlockSpec((1, de, d), weights_of),
        ],
        out_specs=pl.BlockSpec(memory_space=pl.ANY),
        scratch_shapes=[pltpu.VMEM((2, bm * EXPERT_SLAB_ROWS, V7X_LANES), jnp.uint32),
                        pltpu.VMEM((2, bm * EXPERT_SLAB_ROWS, V7X_LANES), jnp.uint32),
                        pltpu.SemaphoreType.DMA,
                        pltpu.SemaphoreType.DMA],
    )
    return pl.pallas_call(
        functools.partial(_experts_kernel, bm=bm),
        out_shape=jax.ShapeDtypeStruct((TOP_K * t + bm, nl, V7X_LANES), jnp.uint32),
        grid_spec=grid_spec,
        compiler_params=_params(("arbitrary",)),
        name="routed_experts",
    )(block_expert, n_valid, n_used, tok_buf, slot_buf, h2, w_gate, w_up, w_down)


def _shared_kernel(h_ref, wg_ref, wu_ref, wd_ref, o_ref):
    x = h_ref[...]
    hid = _silu(jnp.dot(x, wg_ref[...], preferred_element_type=F32)) * jnp.dot(
        x, wu_ref[...], preferred_element_type=F32)
    o_ref[...] = jnp.dot(hid.astype(BF16), wd_ref[...], preferred_element_type=F32).astype(o_ref.dtype)


def _shared(h2, wg, wu, wd):
    t, d = h2.shape
    ds = wg.shape[1]
    tm = _pick(t, 512, 16)
    return pl.pallas_call(
        _shared_kernel,
        out_shape=jax.ShapeDtypeStruct((t, d), F32),
        grid=(t // tm,),
        in_specs=[pl.BlockSpec((tm, d), lambda i: (i, 0)),
                  pl.BlockSpec((d, ds), lambda i: (0, 0)),
                  pl.BlockSpec((d, ds), lambda i: (0, 0)),
                  pl.BlockSpec((ds, d), lambda i: (0, 0))],
        out_specs=pl.BlockSpec((tm, d), lambda i: (i, 0)),
        compiler_params=_params(("parallel",)),
        name="shared_expert",
    )(h2, wg, wu, wd)


def _combine_kernel(x_ref, wt_ref, sh_ref, mod_ref, fw_ref, *rest, gate_idx):
    yk_refs, o_ref = rest[:TOP_K], rest[TOP_K]
    wt = wt_ref[0]
    sh = sh_ref[0]
    tm, d = sh.shape
    nl = d // 2 // V7X_LANES
    lows, highs = [], []
    for c in range(nl):
        lo = sh[:, c * V7X_LANES:(c + 1) * V7X_LANES]
        hi = sh[:, d // 2 + c * V7X_LANES:d // 2 + (c + 1) * V7X_LANES]
        for kk in range(TOP_K):
            y_lo, y_hi = _unpack_halves(yk_refs[kk][pl.ds(c, tm, stride=nl), :])
            lo = lo + wt[:, kk:kk + 1] * y_lo
            hi = hi + wt[:, kk:kk + 1] * y_hi
        lows.append(lo)
        highs.append(hi)
    acc = jnp.concatenate(lows + highs, axis=1)
    m = mod_ref[0]
    x = x_ref[0] + m[gate_idx:gate_idx + 1] * acc
    o_ref[0] = x * lax.rsqrt(jnp.mean(x * x, axis=-1, keepdims=True) + EPS) * fw_ref[...]


def _combine(x1, yk, wtok, shared, mod3, final_w):
    b, n, d = x1.shape
    tm = _pick(n, 128, 16)
    nt = n // tm
    nl = d // 2 // V7X_LANES

    def yk_spec(kk):
        return pl.BlockSpec((tm * nl, V7X_LANES), lambda i, j: ((kk * b + i) * nt + j, 0))

    return pl.pallas_call(
        functools.partial(_combine_kernel, gate_idx=5),
        out_shape=jax.ShapeDtypeStruct((b, n, d), F32),
        grid=(b, nt),
        in_specs=[pl.BlockSpec((1, tm, d), lambda i, j: (i, j, 0)),
                  pl.BlockSpec((1, tm, TOP_K), lambda i, j: (i, j, 0)),
                  pl.BlockSpec((1, tm, d), lambda i, j: (i, j, 0)),
                  pl.BlockSpec((1, 6, d), lambda i, j: (i, 0, 0)),
                  pl.BlockSpec((1, d), lambda i, j: (0, 0))] + [yk_spec(kk) for kk in range(TOP_K)],
        out_specs=pl.BlockSpec((1, tm, d), lambda i, j: (i, j, 0)),
        compiler_params=_params(("parallel", "parallel")),
        name="combine_final_norm",
    )(x1, wtok, shared, mod3, final_w, *([yk] * TOP_K))


def _routing_tables(idx_kt, n_tokens):
    bm = EXPERT_BLOCK_ROWS
    n_assign = TOP_K * n_tokens
    assert n_assign % bm == 0
    n_blocks = n_assign // bm + N_EXPERTS
    e_flat = idx_kt.reshape(n_assign)
    experts = jnp.arange(N_EXPERTS, dtype=I32)
    counts = jnp.sum((e_flat[None, :] == experts[:, None]).astype(I32), axis=1)
    n_pad = (-counts) % bm
    big = 2 * N_EXPERTS
    fill = jnp.arange(bm, dtype=I32)[None, :] < n_pad[:, None]
    fill_keys = jnp.where(fill, 2 * experts[:, None] + 1, big).reshape(N_EXPERTS * bm)
    keys = jnp.concatenate([2 * e_flat, fill_keys])
    vals = jnp.concatenate([jnp.arange(n_assign, dtype=I32), jnp.full((N_EXPERTS * bm,), -1, I32)])
    keys, vals = lax.sort((keys, vals), num_keys=1)
    valid = vals >= 0
    tok = jnp.where(valid, vals % n_tokens, 0)
    slot = jnp.where(valid, vals, n_assign + jnp.arange(n_blocks * bm, dtype=I32) % bm)
    n_valid = jnp.sum(valid.reshape(n_blocks, bm).astype(I32), axis=1)
    n_used = (jnp.sum(counts + n_pad) // bm).astype(I32)
    blk = jnp.minimum(jnp.arange(n_blocks, dtype=I32), n_used - 1)
    block_expert = keys.reshape(n_blocks, bm)[:, 0][blk] // 2
    return (block_expert, n_valid, n_used.reshape(1), tok.reshape(n_blocks, 1, bm),
            slot.reshape(n_blocks, 1, bm))


def kernel(x, c, ctx, c_ctx, w_ada, b_ada, norm_mix, norm_ffn, w_in, conv_a, conv_qkv, a_log, dt_bias,
           gdn_norm, w_proj_a, w_proj_b, w_out, w_router, router_bias, w_gate, w_up, w_down,
           ws_gate, ws_up, ws_down, final_norm):
    assert w_ada.shape[0] == 1, "single (final) layer only"
    b, seq, d = x.shape
    n_ctx = ctx.shape[1]
    t = b * seq
    dc = conv_a.shape[2]
    assert seq % GRID_W == 0 and seq % CHUNK == 0 and n_ctx % CHUNK == 0
    z_off = QKV_DIM + N_GATE_COLS
    conv_off = z_off + V_DIM
    gate_off = conv_off + 3 * dc
    assert w_in.shape[2] == gate_off + 2 * d

    rows = -(-(b + 1) // 8) * 8
    cpad = jnp.zeros((rows, d), F32).at[:b].set(c).at[b].set(c_ctx)
    mod3 = _ada(cpad, w_ada[0], b_ada).reshape(rows, 6, d)

    h = _norm_mod(x, norm_mix, mod3, lambda i: i, 0, 1, BF16).reshape(t, d)
    hc = _norm_mod(ctx, norm_mix, mod3, lambda i: b, 0, 1, BF16).reshape(b * n_ctx, d)
    w_in0 = w_in[0]
    w_qkv = w_in0[:, :QKV_DIM].astype(BF16)
    w_ab = w_in0[:, QKV_DIM:z_off].astype(BF16)
    p_qkv = _mm(h, w_qkv, BF16, "inproj_qkv").reshape(b, seq, QKV_DIM)
    p_ab = _mm(h, w_ab, F32, "inproj_ab").reshape(b, seq, N_GATE_COLS)
    p_z = _mm(h, w_in0[:, z_off:conv_off].astype(BF16), BF16, "inproj_z").reshape(b, seq, V_DIM)
    p_conv = _mm(h, w_in0[:, conv_off:gate_off].astype(BF16), BF16, "inproj_conv")
    p_gate = _mm(h, w_in0[:, gate_off:].astype(BF16), BF16, "inproj_gate").reshape(b, seq, 2 * d)
    pc_qkv = _mm(hc, w_qkv, BF16, "inproj_qkv_ctx").reshape(b, n_ctx, QKV_DIM)
    pc_ab = _mm(hc, w_ab, F32, "inproj_ab_ctx").reshape(b, n_ctx, N_GATE_COLS)

    zero = jnp.zeros((N_V_HEADS,), F32)
    prm = jnp.stack([jnp.concatenate([a_log[0, 0], zero, a_log[0, 1], zero]),
                     jnp.concatenate([dt_bias[0, 0], zero, dt_bias[0, 1], zero])])
    gcol, grow = _gates(p_ab, prm)
    gcol_c, grow_c = _gates(pc_ab, prm)
    yb_in = _gdn(p_qkv, p_z, gcol, grow, pc_qkv, gcol_c, grow_c, conv_qkv[0], gdn_norm)

    ya_in = _conva(p_conv, conv_a[0])
    ya = _mm(ya_in, w_proj_a[0].astype(BF16), BF16, "proj_a").reshape(b, seq, d)
    yb = _mm(yb_in.reshape(t, V_DIM), w_proj_b[0].astype(BF16), BF16, "proj_b").reshape(b, seq, d)
    x1 = _mixout(ya, yb, p_gate, w_out[0].astype(BF16), x, mod3)

    nl = d // 2 // V7X_LANES
    h2_slabs, h2_16, idx_bkt, wgt_bkt = _router(x1, norm_ffn, mod3, w_router[0].T,
                                                router_bias[0].reshape(N_EXPERTS, 1))
    idx_kt = jnp.transpose(idx_bkt, (1, 0, 2)).reshape(TOP_K, t)
    wtok = jnp.transpose(wgt_bkt, (0, 2, 1))
    block_expert, n_valid, n_used, tok_buf, slot_buf = _routing_tables(idx_kt, t)
    yk = _experts(h2_slabs.reshape(t, nl, V7X_LANES), block_expert, n_valid, n_used, tok_buf, slot_buf,
                  w_gate[0], w_up[0], w_down[0])
    shared = _shared(h2_16.reshape(t, d), ws_gate[0].astype(BF16), ws_up[0].astype(BF16),
                     ws_down[0].astype(BF16)).reshape(b, seq, d)
    return _combine(x1, yk.reshape(-1, V7X_LANES), wtok, shared, mod3, final_norm.reshape(1, d))
```

```python
import functools

import jax
import jax.numpy as jnp
from jax import lax
from jax.experimental import pallas as pl
from jax.experimental.pallas import tpu as pltpu

F32 = jnp.float32
BF16 = jnp.bfloat16
I32 = jnp.int32

EPS = 1e-6
GRID_W = 64
N_QK_HEADS = 16
N_V_HEADS = 32
HEAD_K = 128
HEAD_V = 128
CHUNK = 64
N_EXPERTS = 256
N_GROUPS = 8
TOPK_GROUPS = 4
TOP_K = 8
ROUTED_SCALE = 2.5

QK_DIM = N_QK_HEADS * HEAD_K
V_DIM = N_V_HEADS * HEAD_V
QKV_DIM = 2 * QK_DIM + V_DIM
N_GATE_COLS = 4 * N_V_HEADS

V7X_LANES = 128
V7X_VMEM_LIMIT_BYTES = 56 * 1024 * 1024
V7X_MXU_COLS = 256
EXPERT_BLOCK_ROWS = 256
EXPERT_ROW_GROUP = 8
EXPERT_SLAB_ROWS = 9
GDN_LOCAL_GROUP = 4


def _pick(n, target, mult):
    if n <= target:
        return n
    t = (target // mult) * mult
    while t >= mult:
        if n % t == 0:
            return t
        t -= mult
    raise ValueError(f"no tile for {n} (target {target}, multiple {mult})")


def _params(sem, vmem=V7X_VMEM_LIMIT_BYTES):
    return pltpu.CompilerParams(dimension_semantics=sem, vmem_limit_bytes=vmem)


def _sigmoid(x):
    return 1.0 / (1.0 + jnp.exp(-x))


def _silu(x):
    return x * _sigmoid(x)


def _dot(a, b):
    return jnp.dot(a.astype(BF16), b.astype(BF16), preferred_element_type=F32)


def _dot_nt(a, b):
    return lax.dot_general(a.astype(BF16), b.astype(BF16), (((1,), (1,)), ((), ())),
                           preferred_element_type=F32)


def _dot_tn(a, b):
    return lax.dot_general(a.astype(BF16), b.astype(BF16), (((0,), (0,)), ((), ())),
                           preferred_element_type=F32)


def _pack_halves(x):
    n = x.shape[1] // 2
    bits = lax.bitcast_convert_type(x.astype(BF16).astype(F32), jnp.uint32)
    return bits[:, n:] | lax.shift_right_logical(bits[:, :n], jnp.uint32(16))


def _unpack_halves(w):
    lo = lax.bitcast_convert_type(lax.shift_left(w, jnp.uint32(16)), F32)
    hi = lax.bitcast_convert_type(w & jnp.uint32(0xFFFF0000), F32)
    return lo, hi


def _split3(x):
    hi = x.astype(BF16)
    r = x - hi.astype(F32)
    mid = r.astype(BF16)
    lo = (r - mid.astype(F32)).astype(BF16)
    return hi, mid, lo


def _ada_kernel(c_ref, w_ref, b_ref, o_ref):
    a = _silu(c_ref[...])
    o_ref[...] = _dot(a, w_ref[...]) + b_ref[...]


def _ada(cpad, w, b):
    rows, d = cpad.shape
    n = w.shape[1]
    tn = _pick(n, 1024, V7X_LANES)
    return pl.pallas_call(
        _ada_kernel,
        out_shape=jax.ShapeDtypeStruct((rows, n), F32),
        grid=(n // tn,),
        in_specs=[pl.BlockSpec((rows, d), lambda j: (0, 0)),
                  pl.BlockSpec((d, tn), lambda j: (0, j)),
                  pl.BlockSpec((1, tn), lambda j: (0, j))],
        out_specs=pl.BlockSpec((rows, tn), lambda j: (0, j)),
        compiler_params=_params(("parallel",)),
        name="ada_mod",
    )(cpad, w, b)


def _norm_mod_kernel(x_ref, w_ref, mod_ref, o_ref, *, shift_idx, scale_idx):
    x = x_ref[0]
    y = x * lax.rsqrt(jnp.mean(x * x, axis=-1, keepdims=True) + EPS) * w_ref[...]
    m = mod_ref[0]
    o_ref[0] = (y * (1.0 + m[scale_idx:scale_idx + 1]) + m[shift_idx:shift_idx + 1]).astype(o_ref.dtype)


def _norm_mod(x, w, mod3, row_of_batch, shift_idx, scale_idx, out_dtype):
    b, n, d = x.shape
    ts = _pick(n, 512, 16)
    return pl.pallas_call(
        functools.partial(_norm_mod_kernel, shift_idx=shift_idx, scale_idx=scale_idx),
        out_shape=jax.ShapeDtypeStruct((b, n, d), out_dtype),
        grid=(b, n // ts),
        in_specs=[pl.BlockSpec((1, ts, d), lambda i, j: (i, j, 0)),
                  pl.BlockSpec((1, d), lambda i, j: (0, 0)),
                  pl.BlockSpec((1, 6, d), lambda i, j: (row_of_batch(i), 0, 0))],
        out_specs=pl.BlockSpec((1, ts, d), lambda i, j: (i, j, 0)),
        compiler_params=_params(("parallel", "parallel")),
        name="norm_mod",
    )(x, w, mod3)


def _mm_kernel(a_ref, w_ref, o_ref):
    o_ref[...] = jnp.dot(a_ref[...], w_ref[...], preferred_element_type=F32).astype(o_ref.dtype)


def _mm(a, w, out_dtype, name):
    m, k = a.shape
    n = w.shape[1]
    tm = _pick(m, 1024, 16)
    tn = _pick(n, 1024, V7X_LANES)
    return pl.pallas_call(
        _mm_kernel,
        out_shape=jax.ShapeDtypeStruct((m, n), out_dtype),
        grid=(n // tn, m // tm),
        in_specs=[pl.BlockSpec((tm, k), lambda j, i: (i, 0)),
                  pl.BlockSpec((k, tn), lambda j, i: (0, j))],
        out_specs=pl.BlockSpec((tm, tn), lambda j, i: (i, j)),
        compiler_params=_params(("parallel", "parallel")),
        name=name,
    )(a, w)


def _gates_kernel(ab_ref, prm_ref, col_ref, row_ref, *, rows):
    sub = 2 * CHUNK
    n_sub = rows // sub
    prm = prm_ref[...]
    alog = prm[0:1]
    dtb = prm[1:2]
    lane = lax.broadcasted_iota(I32, (sub, N_GATE_COLS), 1)
    is_g = (lane // N_V_HEADS) % 2 == 0
    is_bwd = lane >= 2 * N_V_HEADS
    r = lax.broadcasted_iota(I32, (sub, sub), 0)
    c = lax.broadcasted_iota(I32, (sub, sub), 1)
    same = (r // CHUNK) == (c // CHUNK)
    cum_f = jnp.where(same & (c <= r), 1.0, 0.0).astype(BF16)
    cum_b = jnp.where(same & (c >= r), 1.0, 0.0).astype(BF16)
    for s in range(n_sub):
        x = ab_ref[0, s * sub:(s + 1) * sub, :]
        xs = x + dtb
        softplus = jnp.maximum(xs, 0.0) + jnp.log(1.0 + jnp.exp(-jnp.abs(xs)))
        g = -jnp.exp(alog) * softplus
        beta = _sigmoid(x)
        hi, mid, lo = _split3(g)
        pf = (jnp.dot(cum_f, hi, preferred_element_type=F32) + jnp.dot(cum_f, mid, preferred_element_type=F32)
              + jnp.dot(cum_f, lo, preferred_element_type=F32))
        pb = (jnp.dot(cum_b, hi, preferred_element_type=F32) + jnp.dot(cum_b, mid, preferred_element_type=F32)
              + jnp.dot(cum_b, lo, preferred_element_type=F32))
        out = jnp.where(is_g, jnp.where(is_bwd, pb, pf), beta)
        col_ref[0, s * sub:(s + 1) * sub, :] = out
        for h in range(2):
            oc = out[h * CHUNK:(h + 1) * CHUNK]
            row_ref[0, 2 * s + h] = jnp.concatenate([oc, oc], axis=0).T


def _gates(ab, prm):
    b, n, g = ab.shape
    rows = _pick(n, 512, 2 * CHUNK)
    return pl.pallas_call(
        functools.partial(_gates_kernel, rows=rows),
        out_shape=(jax.ShapeDtypeStruct((b, n, g), F32),
                   jax.ShapeDtypeStruct((b, n // CHUNK, g, 2 * CHUNK), F32)),
        grid=(b, n // rows),
        in_specs=[pl.BlockSpec((1, rows, g), lambda i, j: (i, j, 0)),
                  pl.BlockSpec((2, g), lambda i, j: (0, 0))],
        out_specs=(pl.BlockSpec((1, rows, g), lambda i, j: (i, j, 0)),
                   pl.BlockSpec((1, rows // CHUNK, g, 2 * CHUNK), lambda i, j: (i, j, 0, 0))),
        compiler_params=_params(("parallel", "parallel")),
        name="gate_prep",
    )(ab, prm)


def _conv_silu_tile(src_ref, w, base, rows, n_total):
    x = src_ref[0, pl.ds(base, rows), :].astype(F32)
    pb = pl.multiple_of(jnp.maximum(base - 16, 0), 16)
    nb = pl.multiple_of(jnp.minimum(base + rows, n_total - 16), 16)
    prev = src_ref[0, pl.ds(pb, 16), :].astype(F32)[15:16]
    nxt = src_ref[0, pl.ds(nb, 16), :].astype(F32)[0:1]
    prev = jnp.where(base > 0, prev, 0.0)
    nxt = jnp.where(base + rows < n_total, nxt, 0.0)
    ridx = lax.broadcasted_iota(I32, x.shape, 0)
    xp = jnp.where(ridx == 0, prev, pltpu.roll(x, 1, 0))
    xn = jnp.where(ridx == rows - 1, nxt, pltpu.roll(x, rows - 1, 0))
    y = xp * w[0:1] + x * w[1:2] + xn * w[2:3]
    return _silu(y)


def _l2norm_rows(t):
    return t * lax.rsqrt(jnp.sum(t * t, axis=-1, keepdims=True) + EPS)


def _gdn_kernel(q_ref, k_ref, v_ref, z_ref, gc_ref, gt_ref,
                qc_ref, kc_ref, vc_ref, gcc_ref, gtc_ref,
                cq_ref, ck_ref, cv_ref, gn_ref,
                y_ref,
                qs, ks, vs, osum, st, u_s, w_s, *, n_lat, n_ctx):
    hq = pl.program_id(1)
    rep = N_V_HEADS // N_QK_HEADS
    wq = cq_ref[...]
    wk = ck_ref[...]
    wv = cv_ref[...]

    def prep(src_q, src_k, src_v, n_total, row0):
        tile = _pick(n_total, 256, 16)

        def body(t, carry):
            base = pl.multiple_of(t * tile, tile)
            dst = pl.multiple_of(row0 + base, 16)
            qs[pl.ds(dst, tile), :] = _l2norm_rows(_conv_silu_tile(src_q, wq, base, tile, n_total)) * (HEAD_K ** -0.5)
            ks[pl.ds(dst, tile), :] = _l2norm_rows(_conv_silu_tile(src_k, wk, base, tile, n_total))
            vs[pl.ds(dst, tile), :] = _conv_silu_tile(src_v, wv, base, tile, n_total)
            return carry

        lax.fori_loop(0, n_total // tile, body, 0)

    prep(qc_ref, kc_ref, vc_ref, n_ctx, 0)
    prep(q_ref, k_ref, v_ref, n_lat, n_ctx)

    st[...] = jnp.zeros(st.shape, F32)
    osum[...] = jnp.zeros(osum.shape, F32)

    wide = (CHUNK, 2 * CHUNK)
    ri = lax.broadcasted_iota(I32, wide, 0)
    li = lax.broadcasted_iota(I32, wide, 1)
    ci = li % CHUNK
    left = li < CHUNK
    eye_left = jnp.where(left & (ri == ci), 1.0, 0.0)
    incl = (ri >= ci, ri <= ci)
    strict = (ri > ci, ri < ci)
    last = (CHUNK - 1, 0)
    lane = lax.broadcasted_iota(I32, (CHUNK, N_GATE_COLS), 1)
    zeros_sq = jnp.zeros(wide, BF16)
    zeros_uw = jnp.zeros((CHUNK, 2 * HEAD_V), BF16)

    def gate_col(gtile, col):
        return jnp.sum(jnp.where(lane == col, gtile, 0.0), axis=1, keepdims=True)

    def decay_of(gcol, grow2, dirn):
        return jnp.where(incl[dirn], jnp.exp(jnp.where(incl[dirn], gcol - grow2, 0.0)), 0.0)

    def mm(a, b):
        return jnp.dot(a, b, preferred_element_type=F32)

    grp = GDN_LOCAL_GROUP

    def chunk_of(seq, g, j, dirn):
        n_chunks = seq[3]
        return grp * g + j if dirn == 0 else n_chunks - 1 - grp * g - j

    def local_stages(seq, g, slot):
        gcol_ref, grow_ref, row0, _ = seq
        box = {}

        def prep():
            sbs, rhss, dests = [], [], []
            for j in range(grp):
                for dirn in range(2):
                    c = chunk_of(seq, g, j, dirn)
                    r = pl.multiple_of(row0 + c * CHUNK, CHUNK)
                    k = ks[pl.ds(r, CHUNK), :]
                    vv = vs[pl.ds(r, CHUNK), :]
                    k16 = k.astype(BF16)
                    kk2 = _dot_nt(k16, jnp.concatenate([k16, k16], axis=0))
                    gtile = gcol_ref[0, pl.ds(pl.multiple_of(c * CHUNK, CHUNK), CHUNK), :]
                    for head in range(rep):
                        col_g = (2 * dirn) * N_V_HEADS + hq * rep + head
                        gcol = gate_col(gtile, col_g)
                        bcol = gate_col(gtile, col_g + N_V_HEADS)
                        grow2 = grow_ref[0, c, 2 * dirn, pl.ds(head, 1), :]
                        a2 = bcol * kk2 * decay_of(gcol, grow2, dirn)
                        sbs.append(jnp.where(left, eye_left, jnp.where(strict[dirn], -a2, 0.0)))
                        v = vv[:, head * HEAD_V:(head + 1) * HEAD_V]
                        rhss.append(jnp.concatenate([v * bcol, k * (bcol * jnp.exp(gcol))], axis=1).astype(BF16))
                        dests.append((dirn * rep + head, j))
            box.update(sbs=sbs, rhss=rhss, dests=dests)

        def double():
            nxt = []
            for sb in box["sbs"]:
                sb16 = sb.astype(BF16)
                nxt.append(mm(sb16, jnp.concatenate([zeros_sq, sb16], axis=0)) + jnp.where(left, sb, 0.0))
            box["sbs"] = nxt

        def apply_s32():
            box["sb16s"] = [sb.astype(BF16) for sb in box["sbs"]]
            box["xs"] = [mm(sb16, jnp.concatenate([rhs, zeros_uw], axis=0))
                         for sb16, rhs in zip(box["sb16s"], box["rhss"])]

        def finish():
            for sb16, x, (chain, j) in zip(box["sb16s"], box["xs"], box["dests"]):
                uw = x + mm(sb16, jnp.concatenate([zeros_uw, x.astype(BF16)], axis=0))
                u_s[slot, chain, j * CHUNK:(j + 1) * CHUNK, :] = uw[:, :HEAD_V]
                w_s[slot, chain, j * CHUNK:(j + 1) * CHUNK, :] = uw[:, HEAD_V:].astype(BF16)

        return [prep] + [double] * 5 + [apply_s32, finish]

    def scan_stages(seq, g, j, slot, write_out):
        gcol_ref, grow_ref, row0, _ = seq
        box = {}

        def inputs():
            work = []
            for dirn in range(2):
                c = chunk_of(seq, g, j, dirn)
                r = pl.multiple_of(row0 + c * CHUNK, CHUNK)
                lr = pl.multiple_of(c * CHUNK, CHUNK)
                k = ks[pl.ds(r, CHUNK), :]
                gtile = gcol_ref[0, pl.ds(lr, CHUNK), :]
                if write_out:
                    q = qs[pl.ds(r, CHUNK), :]
                    k16 = k.astype(BF16)
                    qk2 = _dot_nt(q, jnp.concatenate([k16, k16], axis=0))
                for head in range(rep):
                    chain = dirn * rep + head
                    gcol = gate_col(gtile, (2 * dirn) * N_V_HEADS + hq * rep + head)
                    glast = gcol[last[dirn]:last[dirn] + 1, :]
                    kdec16 = (k * jnp.exp(glast - gcol)).astype(BF16)
                    lhs = w_s[slot, chain, j * CHUNK:(j + 1) * CHUNK, :]
                    p16 = None
                    if write_out:
                        grow2 = grow_ref[0, c, 2 * dirn, pl.ds(head, 1), :]
                        p16 = (qk2 * decay_of(gcol, grow2, dirn)).astype(BF16)
                        lhs = jnp.concatenate([lhs, (q * jnp.exp(gcol)).astype(BF16)], axis=0)
                    work.append((chain, lr, head, glast, kdec16, lhs, p16))
            box["work"] = work

        def state_matmul():
            box["s_olds"] = [st[wk[0]] for wk in box["work"]]
            box["tbs"] = [mm(wk[5], s_old.astype(BF16))
                          for wk, s_old in zip(box["work"], box["s_olds"])]

        def update():
            for (chain, lr, head, glast, kdec16, _, p16), s_old, tb in zip(box["work"], box["s_olds"], box["tbs"]):
                vn16 = (u_s[slot, chain, j * CHUNK:(j + 1) * CHUNK, :] - tb[:CHUNK]).astype(BF16)
                if write_out:
                    o = tb[CHUNK:] + mm(p16, jnp.concatenate([vn16, zeros_sq], axis=0))
                    osum[pl.ds(lr, CHUNK), head * HEAD_V:(head + 1) * HEAD_V] += o
                st[chain] = s_old * jnp.exp(glast) + _dot_tn(kdec16, vn16)

        return [inputs, state_matmul, update]

    def run_group(scan_of, local_of):
        local = local_stages(*local_of) if local_of else []
        steps = [scan_stages(scan_of[0], scan_of[1], j, scan_of[2], scan_of[3]) for j in range(grp)] if scan_of else []
        pending = list(local)
        for step in steps:
            step[0]()
            for stage in step[1:]:
                if pending:
                    pending.pop(0)()
                stage()
        for stage in pending:
            stage()

    ctx_seq = (gcc_ref, gtc_ref, 0, n_ctx // CHUNK)
    lat_seq = (gc_ref, gt_ref, n_ctx, n_lat // CHUNK)
    ng_ctx, ng_lat = n_ctx // CHUNK // grp, n_lat // CHUNK // grp
    run_group(None, (ctx_seq, 0, 0))
    for g in range(ng_ctx - 1):
        run_group((ctx_seq, g, g % 2, False), (ctx_seq, g + 1, (g + 1) % 2))
    run_group((ctx_seq, ng_ctx - 1, (ng_ctx - 1) % 2, False), (lat_seq, 0, ng_ctx % 2))

    def lat_body(g, carry):
        run_group((lat_seq, g, (ng_ctx + g) % 2, True), (lat_seq, g + 1, (ng_ctx + g + 1) % 2))
        return carry

    lax.fori_loop(0, ng_lat - 1, lat_body, 0)
    run_group((lat_seq, ng_lat - 1, (ng_ctx + ng_lat - 1) % 2, True), None)

    gn = gn_ref[...]
    tile = _pick(n_lat, 256, 16)

    def epi(t, carry):
        base = pl.multiple_of(t * tile, tile)
        o = osum[pl.ds(base, tile), :]
        z = z_ref[0, pl.ds(base, tile), :].astype(F32)
        for head in range(rep):
            sl = slice(head * HEAD_V, (head + 1) * HEAD_V)
            oh = o[:, sl]
            y = oh * lax.rsqrt(jnp.mean(oh * oh, axis=-1, keepdims=True) + EPS) * gn * _silu(z[:, sl])
            y_ref[0, pl.ds(base, tile), sl] = y.astype(y_ref.dtype)
        return carry

    lax.fori_loop(0, n_lat // tile, epi, 0)


def _gdn(p_qkv, p_z, gcol, grow, pc_qkv, gcol_c, grow_c, conv_qkv, gdn_norm):
    b, n_lat, _ = p_qkv.shape
    n_ctx = pc_qkv.shape[1]
    rep = N_V_HEADS // N_QK_HEADS
    vw = rep * HEAD_V
    kq0 = QK_DIM // HEAD_K
    v0 = 2 * QK_DIM // vw
    nc_lat, nc_ctx = n_lat // CHUNK, n_ctx // CHUNK
    assert nc_lat % GDN_LOCAL_GROUP == 0 and nc_ctx % GDN_LOCAL_GROUP == 0
    assert HEAD_K == HEAD_V == N_GATE_COLS == 2 * CHUNK
    grow5 = grow.reshape(b, nc_lat, 4, N_QK_HEADS, rep, 2 * CHUNK)
    grow5_c = grow_c.reshape(b, nc_ctx, 4, N_QK_HEADS, rep, 2 * CHUNK)

    def seq_specs(n):
        return [pl.BlockSpec((1, n, HEAD_K), lambda i, h: (i, 0, h)),
                pl.BlockSpec((1, n, HEAD_K), lambda i, h: (i, 0, kq0 + h)),
                pl.BlockSpec((1, n, vw), lambda i, h: (i, 0, v0 + h))]

    def gate_specs(n, nc):
        return [pl.BlockSpec((1, n, N_GATE_COLS), lambda i, h: (i, 0, 0)),
                pl.BlockSpec((1, nc, 4, None, rep, 2 * CHUNK), lambda i, h: (i, 0, 0, h, 0, 0))]

    in_specs = (seq_specs(n_lat)
                + [pl.BlockSpec((1, n_lat, vw), lambda i, h: (i, 0, h))]
                + gate_specs(n_lat, nc_lat)
                + seq_specs(n_ctx)
                + gate_specs(n_ctx, nc_ctx)
                + [pl.BlockSpec((3, HEAD_K), lambda i, h: (0, h)),
                   pl.BlockSpec((3, HEAD_K), lambda i, h: (0, kq0 + h)),
                   pl.BlockSpec((3, vw), lambda i, h: (0, v0 + h)),
                   pl.BlockSpec((1, HEAD_V), lambda i, h: (0, 0))])
    n_all = n_ctx + n_lat
    return pl.pallas_call(
        functools.partial(_gdn_kernel, n_lat=n_lat, n_ctx=n_ctx),
        out_shape=jax.ShapeDtypeStruct((b, n_lat, V_DIM), BF16),
        grid=(b, N_QK_HEADS),
        in_specs=in_specs,
        out_specs=pl.BlockSpec((1, n_lat, vw), lambda i, h: (i, 0, h)),
        scratch_shapes=[pltpu.VMEM((n_all, HEAD_K), F32),
                        pltpu.VMEM((n_all, HEAD_K), F32),
                        pltpu.VMEM((n_all, vw), F32),
                        pltpu.VMEM((n_lat, vw), F32),
                        pltpu.VMEM((2 * rep, HEAD_K, HEAD_V), F32),
                        pltpu.VMEM((2, 2 * rep, GDN_LOCAL_GROUP * CHUNK, HEAD_V), F32),
                        pltpu.VMEM((2, 2 * rep, GDN_LOCAL_GROUP * CHUNK, HEAD_K), BF16)],
        compiler_params=_params(("parallel", "parallel")),
        name="gdn_bidir",
    )(p_qkv, p_qkv, p_qkv, p_z, gcol, grow5,
      pc_qkv, pc_qkv, pc_qkv, gcol_c, grow5_c,
      conv_qkv, conv_qkv, conv_qkv, gdn_norm)


def _conva_kernel(gb_ref, gcx_ref, xin_ref, w_ref, o_ref):
    u = gcx_ref[...].astype(F32) * xin_ref[...].astype(F32)
    rows = u.shape[0]
    pos = lax.broadcasted_iota(I32, u.shape, 0) % GRID_W
    up = jnp.where(pos == 0, 0.0, pltpu.roll(u, 1, 0))
    un = jnp.where(pos == GRID_W - 1, 0.0, pltpu.roll(u, rows - 1, 0))
    w = w_ref[...]
    conv = up * w[0:1] + u * w[1:2] + un * w[2:3]
    o_ref[...] = (gb_ref[...].astype(F32) * conv).astype(o_ref.dtype)


def _conva(p_conv, conv_w):
    t, c3 = p_conv.shape
    dc = c3 // 3
    tr = _pick(t, 512, GRID_W)
    tc = _pick(dc, 512, V7X_LANES)
    nj = dc // tc
    return pl.pallas_call(
        _conva_kernel,
        out_shape=jax.ShapeDtypeStruct((t, dc), BF16),
        grid=(t // tr, nj),
        in_specs=[pl.BlockSpec((tr, tc), lambda i, j: (i, j)),
                  pl.BlockSpec((tr, tc), lambda i, j: (i, nj + j)),
                  pl.BlockSpec((tr, tc), lambda i, j: (i, 2 * nj + j)),
                  pl.BlockSpec((3, tc), lambda i, j: (0, j))],
        out_specs=pl.BlockSpec((tr, tc), lambda i, j: (i, j)),
        compiler_params=_params(("parallel", "parallel")),
        name="mixer_a_conv",
    )(p_conv, p_conv, p_conv, conv_w)


def _mixout_kernel(ya_ref, yb_ref, ga_ref, gb_ref, wo_ref, x_ref, mod_ref, o_ref, *, gate_idx):
    mix = (_sigmoid(ga_ref[0].astype(F32)) * ya_ref[0].astype(F32)
           + _sigmoid(gb_ref[0].astype(F32)) * yb_ref[0].astype(F32))
    out = _dot(mix, wo_ref[...])
    m = mod_ref[0]
    o_ref[0] = x_ref[0] + m[gate_idx:gate_idx + 1] * out


def _mixout(ya, yb, p_gate, w_o, x, mod3):
    b, n, d = x.shape
    tm = _pick(n, 512, 16)
    return pl.pallas_call(
        functools.partial(_mixout_kernel, gate_idx=2),
        out_shape=jax.ShapeDtypeStruct((b, n, d), F32),
        grid=(b, n // tm),
        in_specs=[pl.BlockSpec((1, tm, d), lambda i, j: (i, j, 0)),
                  pl.BlockSpec((1, tm, d), lambda i, j: (i, j, 0)),
                  pl.BlockSpec((1, tm, d), lambda i, j: (i, j, 0)),
                  pl.BlockSpec((1, tm, d), lambda i, j: (i, j, 1)),
                  pl.BlockSpec((d, d), lambda i, j: (0, 0)),
                  pl.BlockSpec((1, tm, d), lambda i, j: (i, j, 0)),
                  pl.BlockSpec((1, 6, d), lambda i, j: (i, 0, 0))],
        out_specs=pl.BlockSpec((1, tm, d), lambda i, j: (i, j, 0)),
        compiler_params=_params(("parallel", "parallel")),
        name="merge_outproj",
    )(ya, yb, p_gate, p_gate, w_o, x, mod3)


def _router_kernel(x_ref, w_ref, mod_ref, wr_ref, bias_ref, hs_ref, h16_ref, idx_ref, wgt_ref, *,
                   shift_idx, scale_idx):
    x = x_ref[0]
    y = x * lax.rsqrt(jnp.mean(x * x, axis=-1, keepdims=True) + EPS) * w_ref[...]
    m = mod_ref[0]
    h = y * (1.0 + m[scale_idx:scale_idx + 1]) + m[shift_idx:shift_idx + 1]
    tm, d = h.shape
    h_hi = h.astype(BF16)
    h16_ref[0] = h_hi
    packed = _pack_halves(h_hi)
    npk = d // 2 // V7X_LANES
    for c in range(npk):
        hs_ref[0, pl.ds(c, tm, stride=npk), :] = packed[:, c * V7X_LANES:(c + 1) * V7X_LANES]
    h_lo = (h - h_hi.astype(F32)).astype(BF16)
    wr = wr_ref[...]
    w_hi = wr.astype(BF16)
    w_lo = (wr - w_hi.astype(F32)).astype(BF16)
    logits = _dot_nt(w_hi, h_hi) + _dot_nt(w_hi, h_lo) + _dot_nt(w_lo, h_hi)
    scores = _sigmoid(logits)
    biased = scores + bias_ref[...]
    per_group = N_EXPERTS // N_GROUPS
    neg = -jnp.inf
    i32g = lax.broadcasted_iota(I32, (per_group, tm), 0).astype(F32)
    groups, gscore = [], []
    for g in range(N_GROUPS):
        sg = biased[g * per_group:(g + 1) * per_group, :]
        m1 = jnp.max(sg, axis=0, keepdims=True)
        a1 = jnp.min(jnp.where(sg == m1, i32g, float(per_group)), axis=0, keepdims=True)
        m2 = jnp.max(jnp.where(i32g == a1, neg, sg), axis=0, keepdims=True)
        groups.append(sg)
        gscore.append(m1 + m2)
    parts = []
    for g in range(N_GROUPS):
        ahead = jnp.zeros((1, tm), F32)
        for o in range(N_GROUPS):
            if o == g:
                continue
            beats = (gscore[o] >= gscore[g]) if o < g else (gscore[o] > gscore[g])
            ahead = ahead + jnp.where(beats, 1.0, 0.0)
        parts.append(jnp.where(ahead < float(TOPK_GROUPS), groups[g], neg))
    masked = jnp.concatenate(parts, axis=0)
    ei = lax.broadcasted_iota(I32, masked.shape, 0).astype(F32)
    idxs, wgts = [], []
    for _ in range(TOP_K):
        em = jnp.max(masked, axis=0, keepdims=True)
        ea = jnp.min(jnp.where(masked == em, ei, float(N_EXPERTS)), axis=0, keepdims=True)
        hit = ei == ea
        idxs.append(ea)
        wgts.append(jnp.sum(jnp.where(hit, scores, 0.0), axis=0, keepdims=True))
        masked = jnp.where(hit, neg, masked)
    idx = jnp.concatenate(idxs, axis=0)
    wgt = jnp.concatenate(wgts, axis=0)
    wgt = wgt / jnp.sum(wgt, axis=0, keepdims=True) * ROUTED_SCALE
    idx_ref[0] = idx.astype(I32)
    wgt_ref[0] = wgt


def _router(x1, norm_w, mod3, wr_t, bias_col):
    b, n, d = x1.shape
    tm = _pick(n, 256, V7X_LANES)
    nt = n // tm
    nl = d // 2 // V7X_LANES
    return pl.pallas_call(
        functools.partial(_router_kernel, shift_idx=3, scale_idx=4),
        out_shape=(jax.ShapeDtypeStruct((b, n * nl, V7X_LANES), jnp.uint32),
                   jax.ShapeDtypeStruct((b, n, d), BF16),
                   jax.ShapeDtypeStruct((b, TOP_K, n), I32),
                   jax.ShapeDtypeStruct((b, TOP_K, n), F32)),
        grid=(b, nt),
        in_specs=[pl.BlockSpec((1, tm, d), lambda i, j: (i, j, 0)),
                  pl.BlockSpec((1, d), lambda i, j: (0, 0)),
                  pl.BlockSpec((1, 6, d), lambda i, j: (i, 0, 0)),
                  pl.BlockSpec((N_EXPERTS, d), lambda i, j: (0, 0)),
                  pl.BlockSpec((N_EXPERTS, 1), lambda i, j: (0, 0))],
        out_specs=(pl.BlockSpec((1, tm * nl, V7X_LANES), lambda i, j: (i, j, 0)),
                   pl.BlockSpec((1, tm, d), lambda i, j: (i, j, 0)),
                   pl.BlockSpec((1, TOP_K, tm), lambda i, j: (i, 0, j)),
                   pl.BlockSpec((1, TOP_K, tm), lambda i, j: (i, 0, j))),
        compiler_params=_params(("parallel", "parallel")),
        name="ffn_norm_router",
    )(x1, norm_w, mod3, wr_t, bias_col)


def _spread(total, weights):
    acc, out, lo = 0.0, [], 0
    for w in weights:
        acc += w
        hi = round(total * acc / sum(weights))
        out.append(range(lo, hi))
        lo = hi
    return out


def _experts_kernel(be_ref, nv_ref, nu_ref, tok_ref, slot_ref, h_hbm, wg_ref, wu_ref, wd_ref, y_hbm,
                    xbuf, ybuf, wg16, wu16, wd16, gsem, ssem, *, bm):
    j = pl.program_id(0)
    n_used = nu_ref[0]
    nl = h_hbm.shape[1]
    d = 2 * nl * V7X_LANES
    de = wg_ref.shape[2]
    mxu = V7X_MXU_COLS
    pitch = EXPERT_SLAB_ROWS

    rg = EXPERT_ROW_GROUP
    n_blocks = nv_ref.shape[0]

    def rows_up(n):
        return (n + (rg - 1)) & ~(rg - 1)

    def gather_rows(s):
        return rows_up(jnp.where(s < n_used, nv_ref[jnp.clip(s, 0, n_blocks - 1)], 0))

    def scatter_rows(s):
        real = jnp.where((s >= 2) & (s - 2 < n_used), nv_ref[jnp.clip(s - 2, 0, n_blocks - 1)], 0)
        return rows_up(jnp.where(s == 0, bm, real))

    def wait_step_dmas(s):
        for n, sem in ((gather_rows(s), gsem), (scatter_rows(s), ssem)):
            @pl.when(n > 0)
            def _(n=n, sem=sem):
                pltpu.make_async_copy(h_hbm.at[pl.ds(0, n)], y_hbm.at[pl.ds(0, n)], sem).wait()

    @pl.when(j == 0)
    def _():
        xbuf[...] = jnp.zeros(xbuf.shape, xbuf.dtype)
        ybuf[...] = jnp.zeros(ybuf.shape, ybuf.dtype)

    @pl.when((j >= 1) & (j <= n_used + 1))
    def _():
        wait_step_dmas(j - 1)

    @pl.when((j == 0) | (be_ref[jnp.maximum(j - 1, 0)] != be_ref[jnp.maximum(j - 2, 0)]))
    def _():
        wg16[...] = wg_ref[0].astype(BF16)
        wu16[...] = wu_ref[0].astype(BF16)
        wd16[...] = wd_ref[0].astype(BF16)

    def step(nxt):
        cur = 1 - nxt
        n_gather, n_scatter = gather_rows(j), scatter_rows(j)

        def move_rows(groups):
            for g in groups:
                @pl.when(g * rg < n_gather)
                def _(g=g):
                    for r in range(g * rg, (g + 1) * rg):
                        pltpu.make_async_copy(h_hbm.at[tok_ref[0, 0, r]],
                                              xbuf.at[nxt, pl.ds(r * pitch, nl)], gsem).start()

                @pl.when(g * rg < n_scatter)
                def _(g=g):
                    for r in range(g * rg, (g + 1) * rg):
                        pltpu.make_async_copy(ybuf.at[nxt, pl.ds(r * pitch, nl)],
                                              y_hbm.at[slot_ref[0, 0, r]], ssem).start()

        n_up, n_down = de // mxu, d // mxu
        lanes_per_slice = mxu // V7X_LANES
        parts = _spread(bm // rg, [float(d)] * (2 * n_up) + [float(de)] * n_down)
        halves = [_unpack_halves(xbuf[cur, pl.ds(c, bm, stride=pitch), :]) for c in range(nl)]
        x = jnp.concatenate([lo for lo, _ in halves] + [hi for _, hi in halves], axis=1).astype(BF16)
        hid = []
        for s in range(n_up):
            cols = slice(s * mxu, (s + 1) * mxu)
            g = jnp.dot(x, wg16[:, cols], preferred_element_type=F32)
            move_rows(parts[2 * s])
            u = jnp.dot(x, wu16[:, cols], preferred_element_type=F32)
            move_rows(parts[2 * s + 1])
            hid.append((_silu(g) * u).astype(BF16))
        hid = jnp.concatenate(hid, axis=1)
        for s in range(n_down // 2):
            y_lo = jnp.dot(hid, wd16[:, s * mxu:(s + 1) * mxu], preferred_element_type=F32)
            move_rows(parts[2 * n_up + 2 * s])
            y_hi = jnp.dot(hid, wd16[:, d // 2 + s * mxu:d // 2 + (s + 1) * mxu],
                           preferred_element_type=F32)
            packed = _pack_halves(jnp.concatenate([y_lo, y_hi], axis=1))
            for c in range(lanes_per_slice):
                ybuf[cur, pl.ds(s * lanes_per_slice + c, bm, stride=pitch), :] = (
                    packed[:, c * V7X_LANES:(c + 1) * V7X_LANES])
            move_rows(parts[2 * n_up + 2 * s + 1])

    for parity in range(2):
        @pl.when((j <= n_used + 1) & (j % 2 == parity))
        def _(parity=parity):
            step(parity)

    @pl.when(j == n_used + 1)
    def _():
        wait_step_dmas(j)


def _experts(h2, block_expert, n_valid, n_used, tok_buf, slot_buf, w_gate, w_up, w_down):
    t, nl, _ = h2.shape
    d = 2 * nl * V7X_LANES
    de = w_gate.shape[2]
    n_blocks, _, bm = tok_buf.shape
    assert de % V7X_MXU_COLS == 0 and d % (2 * V7X_MXU_COLS) == 0 and EXPERT_SLAB_ROWS >= nl
    last = n_blocks - 1

    def weights_of(j, be, nv, nu):
        return (be[jnp.maximum(j - 1, 0)], 0, 0)

    assert bm % EXPERT_ROW_GROUP == 0
    grid_spec = pltpu.PrefetchScalarGridSpec(
        num_scalar_prefetch=3,
        grid=(n_blocks + 1,),
        in_specs=[
            pl.BlockSpec((1, 1, bm), lambda j, be, nv, nu: (jnp.minimum(j, last), 0, 0), memory_space=pltpu.SMEM),
            pl.BlockSpec((1, 1, bm), lambda j, be, nv, nu: (jnp.where(j >= 2, jnp.minimum(j - 2, last), last), 0, 0),
                         memory_space=pltpu.SMEM),
            pl.BlockSpec(memory_space=pl.ANY),
            pl.BlockSpec((1, d, de), weights_of),
            pl.BlockSpec((1, d, de), weights_of),
            pl.BlockSpec((1, de, d), weights_of),
        ],
        out_specs=pl.BlockSpec(memory_space=pl.ANY),
        scratch_shapes=[pltpu.VMEM((2, bm * EXPERT_SLAB_ROWS, V7X_LANES), jnp.uint32),
                        pltpu.VMEM((2, bm * EXPERT_SLAB_ROWS, V7X_LANES), jnp.uint32),
                        pltpu.VMEM((d, de), BF16),
                        pltpu.VMEM((d, de), BF16),
                        pltpu.VMEM((de, d), BF16),
                        pltpu.SemaphoreType.DMA,
                        pltpu.SemaphoreType.DMA],
    )
    return pl.pallas_call(
        functools.partial(_experts_kernel, bm=bm),
        out_shape=jax.ShapeDtypeStruct((TOP_K * t + bm, nl, V7X_LANES), jnp.uint32),
        grid_spec=grid_spec,
        compiler_params=_params(("arbitrary",)),
        name="routed_experts",
    )(block_expert, n_valid, n_used, tok_buf, slot_buf, h2, w_gate, w_up, w_down)


def _shared_kernel(h_ref, wg_ref, wu_ref, wd_ref, o_ref):
    x = h_ref[...]
    hid = _silu(jnp.dot(x, wg_ref[...], preferred_element_type=F32)) * jnp.dot(
        x, wu_ref[...], preferred_element_type=F32)
    o_ref[...] = jnp.dot(hid.astype(BF16), wd_ref[...], preferred_element_type=F32).astype(o_ref.dtype)


def _shared(h2, wg, wu, wd):
    t, d = h2.shape
    ds = wg.shape[1]
    tm = _pick(t, 512, 16)
    return pl.pallas_call(
        _shared_kernel,
        out_shape=jax.ShapeDtypeStruct((t, d), F32),
        grid=(t // tm,),
        in_specs=[pl.BlockSpec((tm, d), lambda i: (i, 0)),
                  pl.BlockSpec((d, ds), lambda i: (0, 0)),
                  pl.BlockSpec((d, ds), lambda i: (0, 0)),
                  pl.BlockSpec((ds, d), lambda i: (0, 0))],
        out_specs=pl.BlockSpec((tm, d), lambda i: (i, 0)),
        compiler_params=_params(("parallel",)),
        name="shared_expert",
    )(h2, wg, wu, wd)


def _combine_kernel(x_ref, wt_ref, sh_ref, mod_ref, fw_ref, *rest, gate_idx):
    yk_refs, o_ref = rest[:TOP_K], rest[TOP_K]
    wt = wt_ref[0]
    sh = sh_ref[0]
    tm, d = sh.shape
    nl = d // 2 // V7X_LANES
    lows, highs = [], []
    for c in range(nl):
        lo = sh[:, c * V7X_LANES:(c + 1) * V7X_LANES]
        hi = sh[:, d // 2 + c * V7X_LANES:d // 2 + (c + 1) * V7X_LANES]
        for kk in range(TOP_K):
            y_lo, y_hi = _unpack_halves(yk_refs[kk][pl.ds(c, tm, stride=nl), :])
            lo = lo + wt[:, kk:kk + 1] * y_lo
            hi = hi + wt[:, kk:kk + 1] * y_hi
        lows.append(lo)
        highs.append(hi)
    acc = jnp.concatenate(lows + highs, axis=1)
    m = mod_ref[0]
    x = x_ref[0] + m[gate_idx:gate_idx + 1] * acc
    o_ref[0] = x * lax.rsqrt(jnp.mean(x * x, axis=-1, keepdims=True) + EPS) * fw_ref[...]


def _combine(x1, yk, wtok, shared, mod3, final_w):
    b, n, d = x1.shape
    tm = _pick(n, 128, 16)
    nt = n // tm
    nl = d // 2 // V7X_LANES

    def yk_spec(kk):
        return pl.BlockSpec((tm * nl, V7X_LANES), lambda i, j: ((kk * b + i) * nt + j, 0))

    return pl.pallas_call(
        functools.partial(_combine_kernel, gate_idx=5),
        out_shape=jax.ShapeDtypeStruct((b, n, d), F32),
        grid=(b, nt),
        in_specs=[pl.BlockSpec((1, tm, d), lambda i, j: (i, j, 0)),
                  pl.BlockSpec((1, tm, TOP_K), lambda i, j: (i, j, 0)),
                  pl.BlockSpec((1, tm, d), lambda i, j: (i, j, 0)),
                  pl.BlockSpec((1, 6, d), lambda i, j: (i, 0, 0)),
                  pl.BlockSpec((1, d), lambda i, j: (0, 0))] + [yk_spec(kk) for kk in range(TOP_K)],
        out_specs=pl.BlockSpec((1, tm, d), lambda i, j: (i, j, 0)),
        compiler_params=_params(("parallel", "parallel")),
        name="combine_final_norm",
    )(x1, wtok, shared, mod3, final_w, *([yk] * TOP_K))


def _routing_tables(idx_kt, n_tokens):
    bm = EXPERT_BLOCK_ROWS
    n_assign = TOP_K * n_tokens
    assert n_assign % bm == 0
    n_blocks = n_assign // bm + N_EXPERTS
    e_flat = idx_kt.reshape(n_assign)
    experts = jnp.arange(N_EXPERTS, dtype=I32)
    counts = jnp.sum((e_flat[None, :] == experts[:, None]).astype(I32), axis=1)
    n_pad = (-counts) % bm
    big = 2 * N_EXPERTS
    fill = jnp.arange(bm, dtype=I32)[None, :] < n_pad[:, None]
    fill_keys = jnp.where(fill, 2 * experts[:, None] + 1, big).reshape(N_EXPERTS * bm)
    keys = jnp.concatenate([2 * e_flat, fill_keys])
    vals = jnp.concatenate([jnp.arange(n_assign, dtype=I32), jnp.full((N_EXPERTS * bm,), -1, I32)])
    keys, vals = lax.sort((keys, vals), num_keys=1)
    valid = vals >= 0
    tok = jnp.where(valid, vals % n_tokens, 0)
    slot = jnp.where(valid, vals, n_assign + jnp.arange(n_blocks * bm, dtype=I32) % bm)
    n_valid = jnp.sum(valid.reshape(n_blocks, bm).astype(I32), axis=1)
    n_used = (jnp.sum(counts + n_pad) // bm).astype(I32)
    blk = jnp.minimum(jnp.arange(n_blocks, dtype=I32), n_used - 1)
    block_expert = keys.reshape(n_blocks, bm)[:, 0][blk] // 2
    return (block_expert, n_valid, n_used.reshape(1), tok.reshape(n_blocks, 1, bm),
            slot.reshape(n_blocks, 1, bm))


def kernel(x, c, ctx, c_ctx, w_ada, b_ada, norm_mix, norm_ffn, w_in, conv_a, conv_qkv, a_log, dt_bias,
           gdn_norm, w_proj_a, w_proj_b, w_out, w_router, router_bias, w_gate, w_up, w_down,
           ws_gate, ws_up, ws_down, final_norm):
    assert w_ada.shape[0] == 1, "single (final) layer only"
    b, seq, d = x.shape
    n_ctx = ctx.shape[1]
    t = b * seq
    dc = conv_a.shape[2]
    assert seq % GRID_W == 0 and seq % CHUNK == 0 and n_ctx % CHUNK == 0
    z_off = QKV_DIM + N_GATE_COLS
    conv_off = z_off + V_DIM
    gate_off = conv_off + 3 * dc
    assert w_in.shape[2] == gate_off + 2 * d

    rows = -(-(b + 1) // 8) * 8
    cpad = jnp.zeros((rows, d), F32).at[:b].set(c).at[b].set(c_ctx)
    mod3 = _ada(cpad, w_ada[0], b_ada).reshape(rows, 6, d)

    h = _norm_mod(x, norm_mix, mod3, lambda i: i, 0, 1, BF16).reshape(t, d)
    hc = _norm_mod(ctx, norm_mix, mod3, lambda i: b, 0, 1, BF16).reshape(b * n_ctx, d)
    w_in0 = w_in[0]
    w_qkv = w_in0[:, :QKV_DIM].astype(BF16)
    w_ab = w_in0[:, QKV_DIM:z_off].astype(BF16)
    p_qkv = _mm(h, w_qkv, BF16, "inproj_qkv").reshape(b, seq, QKV_DIM)
    p_ab = _mm(h, w_ab, F32, "inproj_ab").reshape(b, seq, N_GATE_COLS)
    p_z = _mm(h, w_in0[:, z_off:conv_off].astype(BF16), BF16, "inproj_z").reshape(b, seq, V_DIM)
    p_conv = _mm(h, w_in0[:, conv_off:gate_off].astype(BF16), BF16, "inproj_conv")
    p_gate = _mm(h, w_in0[:, gate_off:].astype(BF16), BF16, "inproj_gate").reshape(b, seq, 2 * d)
    pc_qkv = _mm(hc, w_qkv, BF16, "inproj_qkv_ctx").reshape(b, n_ctx, QKV_DIM)
    pc_ab = _mm(hc, w_ab, F32, "inproj_ab_ctx").reshape(b, n_ctx, N_GATE_COLS)

    zero = jnp.zeros((N_V_HEADS,), F32)
    prm = jnp.stack([jnp.concatenate([a_log[0, 0], zero, a_log[0, 1], zero]),
                     jnp.concatenate([dt_bias[0, 0], zero, dt_bias[0, 1], zero])])
    gcol, grow = _gates(p_ab, prm)
    gcol_c, grow_c = _gates(pc_ab, prm)
    yb_in = _gdn(p_qkv, p_z, gcol, grow, pc_qkv, gcol_c, grow_c, conv_qkv[0], gdn_norm)

    ya_in = _conva(p_conv, conv_a[0])
    ya = _mm(ya_in, w_proj_a[0].astype(BF16), BF16, "proj_a").reshape(b, seq, d)
    yb = _mm(yb_in.reshape(t, V_DIM), w_proj_b[0].astype(BF16), BF16, "proj_b").reshape(b, seq, d)
    x1 = _mixout(ya, yb, p_gate, w_out[0].astype(BF16), x, mod3)

    nl = d // 2 // V7X_LANES
    h2_slabs, h2_16, idx_bkt, wgt_bkt = _router(x1, norm_ffn, mod3, w_router[0].T,
                                                router_bias[0].reshape(N_EXPERTS, 1))
    idx_kt = jnp.transpose(idx_bkt, (1, 0, 2)).reshape(TOP_K, t)
    wtok = jnp.transpose(wgt_bkt, (0, 2, 1))
    block_expert, n_valid, n_used, tok_buf, slot_buf = _routing_tables(idx_kt, t)
    yk = _experts(h2_slabs.reshape(t, nl, V7X_LANES), block_expert, n_valid, n_used, tok_buf, slot_buf,
                  w_gate[0], w_up[0], w_down[0])
    shared = _shared(h2_16.reshape(t, d), ws_gate[0].astype(BF16), ws_up[0].astype(BF16),
                     ws_down[0].astype(BF16)).reshape(b, seq, d)
    return _combine(x1, yk.reshape(-1, V7X_LANES), wtok, shared, mod3, final_norm.reshape(1, d))
```
